```python
import math
import jax, jax.numpy as jnp
from jax import lax
import numpy as np

D_MODEL = 1024
BATCH = 8
SEQ = 4096
DEPTH = 4

GRID_W = 64
CTX_LEN = 256
HEAD_DIM = 64
A_HEADS = 4
A_VDIM = 2 * HEAD_DIM
A_WIDTH = A_HEADS * A_VDIM
B_HEADS = 8
B_KV_HEADS = 2
B_REP = B_HEADS // B_KV_HEADS
B_WIDTH = B_HEADS * HEAD_DIM
A_Q_COLS = A_HEADS * 2 * HEAD_DIM
A_K_COLS = A_HEADS * 2 * HEAD_DIM
A_V_COLS = A_WIDTH
B_Q_COLS = B_HEADS * HEAD_DIM
B_K_COLS = B_KV_HEADS * HEAD_DIM
B_V_COLS = B_KV_HEADS * HEAD_DIM
IN_COLS = A_Q_COLS + A_K_COLS + A_V_COLS + B_Q_COLS + B_K_COLS + B_V_COLS
IN_SPLITS = (A_Q_COLS,
             A_Q_COLS + A_K_COLS,
             A_Q_COLS + A_K_COLS + A_V_COLS,
             A_Q_COLS + A_K_COLS + A_V_COLS + B_Q_COLS,
             A_Q_COLS + A_K_COLS + A_V_COLS + B_Q_COLS + B_K_COLS)
MIX_WIDTH = A_WIDTH + B_WIDTH
CONV_WIDTH = 3
MLP_HIDDEN = 4 * D_MODEL
ROPE_THETA = 10000.0
ROPE_FREQS = HEAD_DIM // 4
Q_BLOCK = 128
N_EVEN = (DEPTH + 1) // 2
N_ODD = DEPTH // 2
N_MOD = 6
DEEPNORM_ALPHA = (2 * DEPTH) ** 0.25
DEEPNORM_BETA = (8 * DEPTH) ** -0.25
EPS = 1e-6

kernel_name = "hybrid_diffattn_gqa_shortconv_dit_trunk"


def layer_norm(x, g, b):
    xf = x.astype(jnp.float32)
    mu = jnp.mean(xf, axis=-1, keepdims=True)
    var = jnp.mean(jnp.square(xf - mu), axis=-1, keepdims=True)
    return ((xf - mu) * lax.rsqrt(var + EPS) * g + b).astype(x.dtype)


def rms_norm(x, g):
    xf = x.astype(jnp.float32)
    return (xf * lax.rsqrt(jnp.mean(xf * xf, axis=-1, keepdims=True) + EPS) * g).astype(x.dtype)


def modulate(h, shift, scale):
    return h * (1.0 + scale) + shift


def rope_tables(n_tokens):
    rows = n_tokens // GRID_W
    row = jnp.broadcast_to(jnp.arange(rows)[:, None], (rows, GRID_W)).reshape(-1).astype(jnp.float32)
    col = jnp.broadcast_to(jnp.arange(GRID_W)[None, :], (rows, GRID_W)).reshape(-1).astype(jnp.float32)
    inv_freq = ROPE_THETA ** (-jnp.arange(ROPE_FREQS, dtype=jnp.float32) / ROPE_FREQS)
    ang = jnp.stack([row, col], axis=-1)[:, :, None] * inv_freq
    return jnp.cos(ang), jnp.sin(ang)


def apply_rope_2d(x, cos, sin):
    shp = x.shape
    xs = x.astype(jnp.float32).reshape(*shp[:-1], 2, 2, ROPE_FREQS)
    x1, x2 = xs[..., 0, :], xs[..., 1, :]
    n_head_axes = x.ndim - 3
    c = cos.reshape(cos.shape[0], *([1] * n_head_axes), 2, ROPE_FREQS)
    s = sin.reshape(sin.shape[0], *([1] * n_head_axes), 2, ROPE_FREQS)
    o1 = x1 * c - x2 * s
    o2 = x2 * c + x1 * s
    return jnp.stack([o1, o2], axis=-2).reshape(shp).astype(x.dtype)


def diff_attention(q, k, v, lam, lam_init, subln_g):
    s = jnp.einsum('bqhid,bkhid->bhiqk', q, k) * (HEAD_DIM ** -0.5)
    p = jax.nn.softmax(s.astype(jnp.float32), axis=-1)
    attn = (p[:, :, 0] - lam * p[:, :, 1]).astype(v.dtype)
    o = jnp.einsum('bhqk,bkhe->bqhe', attn, v)
    o = rms_norm(o, subln_g) * (1.0 - lam_init)
    return o.reshape(o.shape[0], o.shape[1], A_WIDTH)


def gqa_attention(q, k, v):
    bsz, nq = q.shape[0], q.shape[1]
    q = q.reshape(bsz, nq, B_KV_HEADS, B_REP, HEAD_DIM)
    s = jnp.einsum('bqgrd,bkgd->bgrqk', q, k) * (HEAD_DIM ** -0.5)
    p = jax.nn.softmax(s.astype(jnp.float32), axis=-1).astype(v.dtype)
    o = jnp.einsum('bgrqk,bkgd->bqgrd', p, v)
    return o.reshape(bsz, nq, B_WIDTH)


def even_mixer(h_lat, h_ctx, w_in, w_out, lam_vecs, subln_g, qn_g, kn_g, lam_init, cos, sin, ctx_out):
    bsz, n_lat = h_lat.shape[0], h_lat.shape[1]

    def project(h, use_rope):
        n = h.shape[1]
        z = h @ w_in
        aq, ak, av, bq, bk, bv = jnp.split(z, IN_SPLITS, axis=-1)
        aq = aq.reshape(bsz, n, A_HEADS, 2, HEAD_DIM)
        ak = ak.reshape(bsz, n, A_HEADS, 2, HEAD_DIM)
        av = av.reshape(bsz, n, A_HEADS, A_VDIM)
        bq = rms_norm(bq.reshape(bsz, n, B_HEADS, HEAD_DIM), qn_g)
        bk = rms_norm(bk.reshape(bsz, n, B_KV_HEADS, HEAD_DIM), kn_g)
        bv = bv.reshape(bsz, n, B_KV_HEADS, HEAD_DIM)
        if use_rope:
            aq = apply_rope_2d(aq, cos, sin)
            ak = apply_rope_2d(ak, cos, sin)
            bq = apply_rope_2d(bq, cos, sin)
            bk = apply_rope_2d(bk, cos, sin)
        return aq, ak, av, bq, bk, bv

    lv = lam_vecs.astype(jnp.float32)
    lam = jnp.exp(jnp.sum(lv[0] * lv[1])) - jnp.exp(jnp.sum(lv[2] * lv[3])) + lam_init

    caq, cak, cav, cbq, cbk, cbv = project(h_ctx, False)
    laq, lak, lav, lbq, lbk, lbv = project(h_lat, True)
    ak_all = jnp.concatenate([cak, lak], axis=1)
    av_all = jnp.concatenate([cav, lav], axis=1)
    bk_all = jnp.concatenate([cbk, lbk], axis=1)
    bv_all = jnp.concatenate([cbv, lbv], axis=1)

    nb = n_lat // Q_BLOCK

    def to_blocks(t):
        return jnp.moveaxis(t.reshape(bsz, nb, Q_BLOCK, *t.shape[2:]), 1, 0)

    def block(qs):
        qa, qb = qs
        return jnp.concatenate([diff_attention(qa, ak_all, av_all, lam, lam_init, subln_g),
                                gqa_attention(qb, bk_all, bv_all)], axis=-1)

    o = lax.map(block, (to_blocks(laq), to_blocks(lbq)))
    o = jnp.moveaxis(o, 0, 1).reshape(bsz, n_lat, MIX_WIDTH)
    y_lat = o @ w_out
    y_ctx = None
    if ctx_out:
        oc = jnp.concatenate([diff_attention(caq, cak, cav, lam, lam_init, subln_g),
                              gqa_attention(cbq, cbk, cbv)], axis=-1)
        y_ctx = oc @ w_out
    return y_lat, y_ctx


def conv_mixer(h, w_in, conv_w, conv_b, w_out):
    n = h.shape[1]
    z = h @ w_in
    gate_b, gate_c, xv = jnp.split(z, 3, axis=-1)
    u = gate_c * xv
    up = jnp.pad(u, ((0, 0), (1, 1), (0, 0)))
    y = up[:, :n] * conv_w[0] + up[:, 1:n + 1] * conv_w[1] + up[:, 2:n + 2] * conv_w[2] + conv_b
    return (gate_b * y) @ w_out


def sq_relu_mlp(h, w1, w2):
    return jnp.square(jax.nn.relu(h @ w1)) @ w2


def setup_inputs(seed: int = 0) -> dict:
    key = jax.random.key(seed)
    ks = jax.random.split(key, 20)
    f32 = jnp.float32
    nrm = lambda k, shp: jax.random.normal(k, shp, f32)
    d = D_MODEL
    return {
        "x": nrm(ks[0], (BATCH, SEQ, d)),
        "c": nrm(ks[1], (BATCH, d)),
        "ctx": nrm(ks[2], (BATCH, CTX_LEN, d)),
        "c_ctx": nrm(ks[3], (d,)),
        "ada_w": nrm(ks[4], (DEPTH, d, N_MOD * d)) * (0.5 * d ** -0.5),
        "ada_b": nrm(ks[5], (DEPTH, N_MOD * d)) * 0.02,
        "attn_w_in": nrm(ks[6], (N_EVEN, d, IN_COLS)) * d ** -0.5,
        "attn_w_out": nrm(ks[7], (N_EVEN, MIX_WIDTH, d)) * (MIX_WIDTH ** -0.5 * DEEPNORM_BETA),
        "diff_lambda": nrm(ks[8], (N_EVEN, 4, HEAD_DIM)) * 0.1,
        "diff_subln_g": 1.0 + 0.02 * nrm(ks[9], (N_EVEN, A_VDIM)),
        "q_norm_g": 1.0 + 0.02 * nrm(ks[10], (N_EVEN, HEAD_DIM)),
        "k_norm_g": 1.0 + 0.02 * nrm(ks[11], (N_EVEN, HEAD_DIM)),
        "conv_w_in": nrm(ks[12], (N_ODD, d, 3 * d)) * d ** -0.5,
        "conv_w": nrm(ks[13], (N_ODD, CONV_WIDTH, d)) * CONV_WIDTH ** -0.5,
        "conv_b": nrm(ks[14], (N_ODD, d)) * 0.02,
        "conv_w_out": nrm(ks[15], (N_ODD, d, d)) * (d ** -0.5 * DEEPNORM_BETA),
        "mlp_w1": nrm(ks[16], (DEPTH, d, MLP_HIDDEN)) * d ** -0.5,
        "mlp_w2": nrm(ks[17], (DEPTH, MLP_HIDDEN, d)) * (MLP_HIDDEN ** -0.5 * DEEPNORM_BETA),
        "ln_g": 1.0 + 0.02 * nrm(ks[18], (DEPTH, 2, d)),
        "ln_b": 0.02 * nrm(ks[19], (DEPTH, 2, d)),
    }


def reference(x, c, ctx, c_ctx, ada_w, ada_b, attn_w_in, attn_w_out, diff_lambda, diff_subln_g,
              q_norm_g, k_norm_g, conv_w_in, conv_w, conv_b, conv_w_out, mlp_w1, mlp_w2, ln_g, ln_b):
    n_lat = x.shape[1]
    cos, sin = rope_tables(n_lat)
    c_act = jax.nn.silu(c)
    cc_act = jax.nn.silu(c_ctx)
    x_ctx = ctx
    for l in range(DEPTH):
        even = (l % 2 == 0)
        ctx_out = any(j % 2 == 0 for j in range(l + 1, DEPTH))
        need_ctx = even or ctx_out
        mod = (c_act @ ada_w[l] + ada_b[l])[:, None, :]
        sh1, sc1, g1, sh2, sc2, g2 = jnp.split(mod, N_MOD, axis=-1)
        h = modulate(x, sh1, sc1)
        if need_ctx:
            mod_c = cc_act @ ada_w[l] + ada_b[l]
            csh1, csc1, cg1, csh2, csc2, cg2 = jnp.split(mod_c, N_MOD, axis=-1)
            hc = modulate(x_ctx, csh1, csc1)
        if even:
            e = l // 2
            lam_init = 0.8 - 0.6 * math.exp(-0.3 * l)
            y, yc = even_mixer(h, hc, attn_w_in[e], attn_w_out[e], diff_lambda[e], diff_subln_g[e],
                               q_norm_g[e], k_norm_g[e], lam_init, cos, sin, ctx_out)
        else:
            o = l // 2
            y = conv_mixer(h, conv_w_in[o], conv_w[o], conv_b[o], conv_w_out[o])
            yc = conv_mixer(hc, conv_w_in[o], conv_w[o], conv_b[o], conv_w_out[o]) if ctx_out else None
        x = layer_norm(DEEPNORM_ALPHA * x + g1 * y, ln_g[l, 0], ln_b[l, 0])
        h = modulate(x, sh2, sc2)
        x = layer_norm(DEEPNORM_ALPHA * x + g2 * sq_relu_mlp(h, mlp_w1[l], mlp_w2[l]), ln_g[l, 1], ln_b[l, 1])
        if ctx_out:
            x_ctx = layer_norm(DEEPNORM_ALPHA * x_ctx + cg1 * yc, ln_g[l, 0], ln_b[l, 0])
            hc = modulate(x_ctx, csh2, csc2)
            x_ctx = layer_norm(DEEPNORM_ALPHA * x_ctx + cg2 * sq_relu_mlp(hc, mlp_w1[l], mlp_w2[l]),
                               ln_g[l, 1], ln_b[l, 1])
    return x
```

```python
import functools
import math

import numpy as np
import jax
import jax.numpy as jnp
from jax import lax
from jax.experimental import pallas as pl
from jax.experimental.pallas import tpu as pltpu

F32 = jnp.float32
BF16 = jnp.bfloat16

D_MODEL = 1024
CTX_LEN = 256
GRID_W = 64
HEAD_DIM = 64
A_HEADS = 4
A_VDIM = 2 * HEAD_DIM
B_HEADS = 8
B_KV_HEADS = 2
Q_COLS = 1024
K_COLS = 640
V_COLS = 640
MLP_HIDDEN = 4 * D_MODEL
ROPE_THETA = 10000.0
ROPE_FREQS = HEAD_DIM // 4
N_MOD = 6
EPS = 1e-6

TILE = 256
LANES = 128
HALO = 8
HID_CHUNK = 1024
NEG_BIG = -1e30
VMEM_LIMIT = 56 * 1024 * 1024


def _const_spec(shape):
    nd = len(shape)
    return pl.BlockSpec(shape, lambda *_: (0,) * nd, pipeline_mode=pl.Buffered(1))


def _params(*sem):
    return pltpu.CompilerParams(dimension_semantics=sem, vmem_limit_bytes=VMEM_LIMIT)


def _ada_kernel(c_ref, w_ref, b_ref, o_ref):
    c = c_ref[...]
    act = c * (1.0 / (1.0 + jnp.exp(-c)))
    o_ref[...] = jnp.dot(act.astype(BF16), w_ref[...].astype(BF16),
                         preferred_element_type=F32) + b_ref[...]


def _ada_modulation(c_all, ada_w, ada_b):
    depth, d, n = ada_w.shape
    rows = c_all.shape[0]
    tn = 1536
    return pl.pallas_call(
        _ada_kernel,
        grid=(depth, n // tn),
        in_specs=[pl.BlockSpec((rows, d), lambda l, j: (0, 0)),
                  pl.BlockSpec((None, d, tn), lambda l, j: (l, 0, j)),
                  pl.BlockSpec((None, 1, tn), lambda l, j: (l, 0, j))],
        out_specs=pl.BlockSpec((None, rows, tn), lambda l, j: (l, 0, j)),
        out_shape=jax.ShapeDtypeStruct((depth, rows, n), F32),
        compiler_params=_params("arbitrary", "arbitrary"),
    )(c_all, ada_w, ada_b.reshape(depth, 1, n))


def _layer_norm(v, g, b):
    mu = jnp.mean(v, axis=-1, keepdims=True)
    d = v - mu
    var = jnp.mean(d * d, axis=-1, keepdims=True)
    return d * lax.rsqrt(var + EPS) * g + b


def _post_mixer(x, y, mod_ref, ln_ref, w1_ref, w2_ref, alpha):
    x1 = _layer_norm(alpha * x + mod_ref[2:3, :] * y, ln_ref[0:1, :], ln_ref[1:2, :])
    h2 = (x1 * (1.0 + mod_ref[4:5, :]) + mod_ref[3:4, :]).astype(BF16)
    y2 = jnp.zeros_like(x1)
    for c in range(MLP_HIDDEN // HID_CHUNK):
        sl = slice(c * HID_CHUNK, (c + 1) * HID_CHUNK)
        hid = jnp.maximum(jnp.dot(h2, w1_ref[:, sl], preferred_element_type=F32), 0.0)
        y2 = y2 + jnp.dot((hid * hid).astype(BF16), w2_ref[sl, :], preferred_element_type=F32)
    return _layer_norm(alpha * x1 + mod_ref[5:6, :] * y2, ln_ref[2:3, :], ln_ref[3:4, :])


def _mod_row_map(n_batch, has_ctx):
    if has_ctx:
        return lambda b, i: (jnp.where(i == 0, n_batch, b), 0, 0)
    return lambda b, i: (b, 0, 0)


def _rope(z, cos, s_up, s_dn):
    n = z.shape[-1]
    return z * cos + pltpu.roll(z, n - ROPE_FREQS, 1) * s_up + pltpu.roll(z, ROPE_FREQS, 1) * s_dn


def _group_mean_sq(z, gsum_ref):
    z2 = z * z
    hi = z2.astype(BF16)
    lo = (z2 - hi.astype(F32)).astype(BF16)
    g = gsum_ref[...]
    ss = jnp.dot(hi, g, preferred_element_type=F32) + jnp.dot(lo, g, preferred_element_type=F32)
    return ss * (1.0 / HEAD_DIM)


def _proj_kernel(x_ref, mod_ref, w_ref, cos_ref, sup_ref, sdn_ref, qg_ref, kg_ref, gsum_ref,
                 q_ref, k_ref, vt_ref):
    h = (x_ref[...] * (1.0 + mod_ref[1:2, :]) + mod_ref[0:1, :]).astype(BF16)
    cos, s_up, s_dn = cos_ref[...], sup_ref[...], sdn_ref[...]
    scale = HEAD_DIM ** -0.5

    def lane_groups(z, fn):
        return [fn(z[:, g * LANES:(g + 1) * LANES]) for g in range(z.shape[1] // LANES)]

    def rope_all(z, mul):
        return jnp.concatenate(lane_groups(z, lambda t: _rope(t, cos, s_up, s_dn) * mul), axis=1)

    def norm_all(z, gain_ref):
        gain = gain_ref[...]
        return jnp.concatenate(
            lane_groups(z, lambda t: t * lax.rsqrt(_group_mean_sq(t, gsum_ref) + EPS) * gain), axis=1)

    aq = jnp.dot(h, w_ref[:, 0:512], preferred_element_type=F32)
    q_ref[:, 0:512] = rope_all(aq, scale).astype(BF16)
    bq = jnp.dot(h, w_ref[:, 512:1024], preferred_element_type=F32)
    q_ref[:, 512:1024] = rope_all(norm_all(bq, qg_ref), scale).astype(BF16)
    ak = jnp.dot(h, w_ref[:, 1024:1536], preferred_element_type=F32)
    k_ref[:, 0:512] = rope_all(ak, 1.0).astype(BF16)
    bk = jnp.dot(h, w_ref[:, 1536:1664], preferred_element_type=F32)
    k_ref[:, 512:640] = rope_all(norm_all(bk, kg_ref), 1.0).astype(BF16)
    v = jnp.dot(h, w_ref[:, 1664:2304], preferred_element_type=F32)
    vt_ref[...] = v.T.astype(BF16)


def _even_projection(xs, mod_l, w_in, cos_t, sup_t, sdn_t, qg, kg, gsum, n_batch):
    bsz, t_all, d = xs.shape
    nt = t_all // TILE
    return pl.pallas_call(
        _proj_kernel,
        grid=(bsz, nt),
        in_specs=[pl.BlockSpec((None, TILE, d), lambda b, i: (b, i, 0)),
                  pl.BlockSpec((None, N_MOD, d), _mod_row_map(n_batch, True)),
                  _const_spec(w_in.shape),
                  pl.BlockSpec((TILE, LANES), lambda b, i: (i, 0)),
                  pl.BlockSpec((TILE, LANES), lambda b, i: (i, 0)),
                  pl.BlockSpec((TILE, LANES), lambda b, i: (i, 0)),
                  _const_spec(qg.shape), _const_spec(kg.shape), _const_spec(gsum.shape)],
        out_specs=[pl.BlockSpec((None, TILE, Q_COLS), lambda b, i: (b, i, 0)),
                   pl.BlockSpec((None, TILE, K_COLS), lambda b, i: (b, i, 0)),
                   pl.BlockSpec((None, None, V_COLS, TILE), lambda b, i: (b, i, 0, 0))],
        out_shape=[jax.ShapeDtypeStruct((bsz, t_all, Q_COLS), BF16),
                   jax.ShapeDtypeStruct((bsz, t_all, K_COLS), BF16),
                   jax.ShapeDtypeStruct((bsz, nt, V_COLS, TILE), BF16)],
        compiler_params=_params("arbitrary", "arbitrary"),
    )(xs, mod_l, w_in, cos_t, sup_t, sdn_t, qg, kg, gsum)


def _attn_kernel(q_ref, k_ref, vt_ref, lamv_ref, subg_ref, o_ref, ot_ref, *, q_off, n_key_tiles,
                 lam_init):
    if q_off == 0:
        n_k = jnp.where(pl.program_id(1) == 0, 1, n_key_tiles)
    else:
        n_k = n_key_tiles
    lower = lax.broadcasted_iota(jnp.int32, (TILE, LANES), 1) < HEAD_DIM
    zero = jnp.zeros((TILE, LANES), BF16)

    lv = lamv_ref[...]
    lam = (jnp.exp(jnp.sum(lv[0:1] * lv[1:2], axis=-1, keepdims=True))
           - jnp.exp(jnp.sum(lv[2:3] * lv[3:4], axis=-1, keepdims=True)) + lam_init)

    def pair_scores(q_cols, k_cols, v_rows):
        qt = q_ref[:, q_cols]
        qcat = jnp.concatenate([jnp.where(lower, qt, zero), jnp.where(lower, zero, qt)], axis=0)
        n_v = [r.stop - r.start for r in v_rows]

        def body(c, carry):
            m, l, accs = carry
            start = pl.multiple_of(c * TILE, TILE)
            kc = k_ref[pl.ds(start, TILE), k_cols]
            s = lax.dot_general(kc, qcat, (((1,), (1,)), ((), ())), preferred_element_type=F32)
            m_new = jnp.maximum(m, jnp.max(s, axis=0, keepdims=True))
            alpha = jnp.exp(m - m_new)
            p = jnp.exp(s - m_new)
            l = alpha * l + jnp.sum(p, axis=0, keepdims=True)
            pb = p.astype(BF16)
            new_accs = []
            for half, (rows, acc) in enumerate(zip(v_rows, accs)):
                cs = slice(half * TILE, (half + 1) * TILE)
                vc = vt_ref[c, rows, :]
                new_accs.append(alpha[:, cs] * acc
                                + jnp.dot(vc, pb[:, cs], preferred_element_type=F32))
            return m_new, l, tuple(new_accs)

        init = (jnp.full((1, 2 * TILE), NEG_BIG, F32), jnp.zeros((1, 2 * TILE), F32),
                tuple(jnp.zeros((n, TILE), F32) for n in n_v))
        _, l, accs = lax.fori_loop(0, n_k, body, init)
        inv = 1.0 / l
        return [acc * inv[:, half * TILE:(half + 1) * TILE] for half, acc in enumerate(accs)]

    subg = subg_ref[...]
    for hd in range(A_HEADS):
        cols = slice(hd * LANES, (hd + 1) * LANES)
        rows = slice(hd * A_VDIM, (hd + 1) * A_VDIM)
        o1, o2 = pair_scores(cols, cols, [rows, rows])
        o = o1 - lam * o2
        ms = jnp.mean(o * o, axis=0, keepdims=True)
        ot_ref[rows, :] = o * lax.rsqrt(ms + EPS) * subg * (1.0 - lam_init)
    kv0 = A_HEADS * A_VDIM
    for j in range(B_HEADS // 2):
        cols = slice(kv0 + j * LANES, kv0 + (j + 1) * LANES)
        oa, ob = pair_scores(cols, slice(kv0, kv0 + LANES),
                             [slice(kv0, kv0 + HEAD_DIM), slice(kv0 + HEAD_DIM, kv0 + 2 * HEAD_DIM)])
        ot_ref[kv0 + j * LANES:kv0 + j * LANES + HEAD_DIM, :] = oa
        ot_ref[kv0 + j * LANES + HEAD_DIM:kv0 + (j + 1) * LANES, :] = ob
    o_ref[...] = ot_ref[...].T.astype(BF16)


def _attention(q_all, k_all, vt_all, lam_vecs, subg, q_off, lam_init):
    bsz, t_all, _ = q_all.shape
    nt = t_all // TILE
    nq = nt - q_off
    kern = functools.partial(_attn_kernel, q_off=q_off, n_key_tiles=nt, lam_init=lam_init)
    return pl.pallas_call(
        kern,
        grid=(bsz, nq),
        in_specs=[pl.BlockSpec((None, TILE, Q_COLS), lambda b, i: (b, i + q_off, 0)),
                  pl.BlockSpec((None, t_all, K_COLS), lambda b, i: (b, 0, 0)),
                  pl.BlockSpec((None, nt, V_COLS, TILE), lambda b, i: (b, 0, 0, 0)),
                  _const_spec(lam_vecs.shape), _const_spec(subg.shape)],
        out_specs=pl.BlockSpec((None, TILE, Q_COLS), lambda b, i: (b, i, 0)),
        out_shape=jax.ShapeDtypeStruct((bsz, nq * TILE, Q_COLS), BF16),
        scratch_shapes=[pltpu.VMEM((Q_COLS, TILE), F32)],
        compiler_params=_params("arbitrary", "arbitrary"),
    )(q_all, k_all, vt_all, lam_vecs, subg)


def _post_kernel(o_ref, x_ref, mod_ref, ln_ref, wo_ref, w1_ref, w2_ref, out_ref, *, alpha):
    y = jnp.dot(o_ref[...], wo_ref[...], preferred_element_type=F32)
    out_ref[...] = _post_mixer(x_ref[...], y, mod_ref, ln_ref, w1_ref, w2_ref, alpha)


def _even_post(o, xs, mod_l, ln_l, w_out, w1, w2, x_off, n_batch, alpha):
    bsz, t_o, d = o.shape
    nt = t_o // TILE
    kern = functools.partial(_post_kernel, alpha=alpha)
    return pl.pallas_call(
        kern,
        grid=(bsz, nt),
        in_specs=[pl.BlockSpec((None, TILE, d), lambda b, i: (b, i, 0)),
                  pl.BlockSpec((None, TILE, d), lambda b, i: (b, i + x_off, 0)),
                  pl.BlockSpec((None, N_MOD, d), _mod_row_map(n_batch, x_off == 0)),
                  _const_spec(ln_l.shape), _const_spec(w_out.shape),
                  _const_spec(w1.shape), _const_spec(w2.shape)],
        out_specs=pl.BlockSpec((None, TILE, d), lambda b, i: (b, i, 0)),
        out_shape=jax.ShapeDtypeStruct((bsz, t_o, d), F32),
        compiler_params=_params("arbitrary", "arbitrary"),
    )(o, xs, mod_l, ln_l, w_out, w1, w2)


def _odd_kernel(x_ref, xp_ref, xn_ref, mod_ref, ln_ref, wi_ref, cw_ref, cb_ref, wo_ref, w1_ref, w2_ref,
                out_ref, u_ref, *, alpha, has_ctx, n_tiles):
    i = pl.program_id(1)
    if has_ctx:
        first = i <= 1
        last = jnp.logical_or(i == 0, i == n_tiles - 1)
    else:
        first = i == 0
        last = i == n_tiles - 1
    x = x_ref[...]
    shift, scale1 = mod_ref[0:1, :], 1.0 + mod_ref[1:2, :]
    x_ext = jnp.concatenate([xp_ref[...], x, xn_ref[...]], axis=0)
    h_ext = (x_ext * scale1 + shift).astype(BF16)
    z_c = jnp.dot(h_ext, wi_ref[:, D_MODEL:2 * D_MODEL], preferred_element_type=F32)
    z_x = jnp.dot(h_ext, wi_ref[:, 2 * D_MODEL:3 * D_MODEL], preferred_element_type=F32)
    row = lax.broadcasted_iota(jnp.int32, (TILE + 2 * HALO, 1), 0)
    live_from = jnp.where(first, HALO, 0)
    live_to = jnp.where(last, HALO + TILE, TILE + 2 * HALO)
    live = jnp.logical_and(row >= live_from, row < live_to)
    u_ref[...] = jnp.where(live, z_c * z_x, 0.0)
    conv = (u_ref[HALO - 1:HALO - 1 + TILE, :] * cw_ref[0:1, :]
            + u_ref[HALO:HALO + TILE, :] * cw_ref[1:2, :]
            + u_ref[HALO + 1:HALO + 1 + TILE, :] * cw_ref[2:3, :] + cb_ref[...])
    h = (x * scale1 + shift).astype(BF16)
    z_b = jnp.dot(h, wi_ref[:, 0:D_MODEL], preferred_element_type=F32)
    y = jnp.dot((z_b * conv).astype(BF16), wo_ref[...], preferred_element_type=F32)
    out_ref[...] = _post_mixer(x, y, mod_ref, ln_ref, w1_ref, w2_ref, alpha)


def _odd_layer(xs, mod_l, ln_l, w_in, conv_w, conv_b, w_out, w1, w2, has_ctx, n_batch, alpha):
    bsz, t_all, d = xs.shape
    nt = t_all // TILE
    per_tile = TILE // HALO
    n_halo_blocks = t_all // HALO
    kern = functools.partial(_odd_kernel, alpha=alpha, has_ctx=has_ctx, n_tiles=nt)
    return pl.pallas_call(
        kern,
        grid=(bsz, nt),
        in_specs=[pl.BlockSpec((None, TILE, d), lambda b, i: (b, i, 0)),
                  pl.BlockSpec((None, HALO, d), lambda b, i: (b, jnp.maximum(i * per_tile - 1, 0), 0)),
                  pl.BlockSpec((None, HALO, d),
                               lambda b, i: (b, jnp.minimum((i + 1) * per_tile, n_halo_blocks - 1), 0)),
                  pl.BlockSpec((None, N_MOD, d), _mod_row_map(n_batch, has_ctx)),
                  _const_spec(ln_l.shape), _const_spec(w_in.shape), _const_spec(conv_w.shape),
                  _const_spec(conv_b.shape), _const_spec(w_out.shape),
                  _const_spec(w1.shape), _const_spec(w2.shape)],
        out_specs=pl.BlockSpec((None, TILE, d), lambda b, i: (b, i, 0)),
        out_shape=jax.ShapeDtypeStruct((bsz, t_all, d), F32),
        scratch_shapes=[pltpu.VMEM((TILE + 2 * HALO, d), F32)],
        compiler_params=_params("arbitrary", "arbitrary"),
    )(xs, xs, xs, mod_l, ln_l, w_in, conv_w, conv_b, w_out, w1, w2)


def _rope_tables(n_lat):
    t = jnp.arange(n_lat)
    pos = jnp.stack([(t // GRID_W).astype(F32), (t % GRID_W).astype(F32)], axis=-1)
    inv_freq = ROPE_THETA ** (-jnp.arange(ROPE_FREQS, dtype=F32) / ROPE_FREQS)
    ang = pos[:, :, None] * inv_freq
    cos = jnp.repeat(jnp.cos(ang), 2, axis=1).reshape(n_lat, HEAD_DIM)
    sin = jnp.repeat(jnp.sin(ang), 2, axis=1).reshape(n_lat, HEAD_DIM)
    first_half = (np.arange(HEAD_DIM) % (2 * ROPE_FREQS)) < ROPE_FREQS
    s_up = jnp.where(first_half, -sin, 0.0)
    s_dn = jnp.where(first_half, 0.0, sin)
    reps = LANES // HEAD_DIM
    pad = lambda a, fill: jnp.concatenate(
        [jnp.full((CTX_LEN, LANES), fill, F32), jnp.tile(a, (1, reps))], axis=0)
    return pad(cos, 1.0), pad(s_up, 0.0), pad(s_dn, 0.0)


def _gqa_head_order():
    half = B_HEADS // 2
    return [h for j in range(half) for h in (j, j + half)]


def kernel(x, c, ctx, c_ctx, ada_w, ada_b, attn_w_in, attn_w_out, diff_lambda, diff_subln_g, q_norm_g, k_norm_g, conv_w_in, conv_w, conv_b, conv_w_out, mlp_w1, mlp_w2, ln_g, ln_b):
    bsz, n_lat, d = x.shape
    depth = ada_w.shape[0]
    assert d == D_MODEL and ctx.shape[1] == CTX_LEN == TILE and n_lat % TILE == 0
    alpha = (2 * depth) ** 0.25

    n_rows = -(-(bsz + 1) // 8) * 8
    c_all = jnp.concatenate([c, c_ctx[None, :], jnp.zeros((n_rows - bsz - 1, d), F32)], axis=0)
    mod = _ada_modulation(c_all, ada_w, ada_b).reshape(depth, n_rows, N_MOD, d)

    cos_t, sup_t, sdn_t = _rope_tables(n_lat)
    order = _gqa_head_order()
    bq_cols = np.concatenate([1536 + HEAD_DIM * h + np.arange(HEAD_DIM) for h in order])
    in_cols = np.concatenate([np.arange(0, 512), bq_cols, np.arange(512, 1024), np.arange(2048, 2176),
                              np.arange(1024, 1536), np.arange(2176, 2304)])
    out_rows = np.concatenate([np.arange(0, 512)] + [512 + HEAD_DIM * h + np.arange(HEAD_DIM) for h in order])
    gsum = jnp.asarray(np.kron(np.eye(LANES // HEAD_DIM), np.ones((HEAD_DIM, HEAD_DIM))), BF16)
    ln = jnp.stack([ln_g[:, 0], ln_b[:, 0], ln_g[:, 1], ln_b[:, 1]], axis=1)

    xs = jnp.concatenate([ctx, x], axis=1)
    has_ctx = True
    for l in range(depth):
        ctx_out = any(j % 2 == 0 for j in range(l + 1, depth))
        w1 = mlp_w1[l].astype(BF16)
        w2 = mlp_w2[l].astype(BF16)
        if l % 2 == 0:
            e = l // 2
            assert has_ctx
            lam_init = 0.8 - 0.6 * math.exp(-0.3 * l)
            w_in = attn_w_in[e][:, in_cols].astype(BF16)
            w_out = attn_w_out[e][out_rows, :].astype(BF16)
            tile2 = lambda g: jnp.tile(g, LANES // HEAD_DIM)[None, :]
            q_all, k_all, vt_all = _even_projection(xs, mod[l], w_in, cos_t, sup_t, sdn_t,
                                                    tile2(q_norm_g[e]), tile2(k_norm_g[e]), gsum, bsz)
            q_off = 0 if ctx_out else 1
            o = _attention(q_all, k_all, vt_all, diff_lambda[e], diff_subln_g[e][:, None], q_off, lam_init)
            xs = _even_post(o, xs, mod[l], ln[l], w_out, w1, w2, q_off, bsz, alpha)
            has_ctx = ctx_out
        else:
            od = l // 2
            if has_ctx and not ctx_out:
                xs = xs[:, CTX_LEN:]
                has_ctx = False
            xs = _odd_layer(xs, mod[l], ln[l], conv_w_in[od].astype(BF16), conv_w[od], conv_b[od][None, :],
                            conv_w_out[od].astype(BF16), w1, w2, has_ctx, bsz, alpha)
    return xs[:, CTX_LEN:] if has_ctx else xs
```

```python
import functools
import math

import numpy as np
import jax
import jax.numpy as jnp
from jax import lax
from jax.experimental import pallas as pl
from jax.experimental.pallas import tpu as pltpu

F32 = jnp.float32
BF16 = jnp.bfloat16

D_MODEL = 1024
CTX_LEN = 256
GRID_W = 64
HEAD_DIM = 64
A_HEADS = 4
A_VDIM = 2 * HEAD_DIM
B_HEADS = 8
B_KV_HEADS = 2
Q_COLS = 1024
K_COLS = 640
V_COLS = 640
ONES_ROWS = 16
MLP_HIDDEN = 4 * D_MODEL
ROPE_THETA = 10000.0
ROPE_FREQS = HEAD_DIM // 4
N_MOD = 6
EPS = 1e-6

TILE = 256
LANES = 128
SUBLANES = 8
HALO = SUBLANES
HID_CHUNK = 1024
MAX_KEY_CHUNK = 2176
NEG_BIG = -1e30
VMEM_LIMIT = 56 * 1024 * 1024


def _const_spec(shape):
    nd = len(shape)
    return pl.BlockSpec(shape, lambda *_: (0,) * nd, pipeline_mode=pl.Buffered(1))


def _params(*sem):
    return pltpu.CompilerParams(dimension_semantics=sem, vmem_limit_bytes=VMEM_LIMIT)


def _ada_kernel(c_ref, w_ref, b_ref, o_ref):
    c = c_ref[...]
    act = c * (1.0 / (1.0 + jnp.exp(-c)))
    o_ref[...] = jnp.dot(act.astype(BF16), w_ref[...].astype(BF16),
                         preferred_element_type=F32) + b_ref[...]


def _ada_modulation(c_all, ada_w, ada_b):
    depth, d, n = ada_w.shape
    rows = c_all.shape[0]
    tn = 1536
    return pl.pallas_call(
        _ada_kernel,
        grid=(depth, n // tn),
        in_specs=[pl.BlockSpec((rows, d), lambda l, j: (0, 0)),
                  pl.BlockSpec((None, d, tn), lambda l, j: (l, 0, j)),
                  pl.BlockSpec((None, 1, tn), lambda l, j: (l, 0, j))],
        out_specs=pl.BlockSpec((None, rows, tn), lambda l, j: (l, 0, j)),
        out_shape=jax.ShapeDtypeStruct((depth, rows, n), F32),
        compiler_params=_params("arbitrary", "arbitrary"),
    )(c_all, ada_w, ada_b.reshape(depth, 1, n))


def _layer_norm(v, g, b):
    mu = jnp.mean(v, axis=-1, keepdims=True)
    d = v - mu
    var = jnp.mean(d * d, axis=-1, keepdims=True)
    return d * lax.rsqrt(var + EPS) * g + b


def _post_mixer(x, y, mod_ref, ln_ref, w1_ref, w2_ref, alpha):
    x1 = _layer_norm(alpha * x + mod_ref[2:3, :] * y, ln_ref[0:1, :], ln_ref[1:2, :])
    h2 = (x1 * (1.0 + mod_ref[4:5, :]) + mod_ref[3:4, :]).astype(BF16)
    y2 = jnp.zeros_like(x1)
    for c in range(MLP_HIDDEN // HID_CHUNK):
        sl = slice(c * HID_CHUNK, (c + 1) * HID_CHUNK)
        hid = jnp.maximum(jnp.dot(h2, w1_ref[:, sl], preferred_element_type=F32), 0.0)
        y2 = y2 + jnp.dot((hid * hid).astype(BF16), w2_ref[sl, :], preferred_element_type=F32)
    return _layer_norm(alpha * x1 + mod_ref[5:6, :] * y2, ln_ref[2:3, :], ln_ref[3:4, :])


def _mod_row_map(n_batch, has_ctx):
    if has_ctx:
        return lambda b, i: (jnp.where(i == 0, n_batch, b), 0, 0)
    return lambda b, i: (b, 0, 0)


def _vt_row_layout():
    blocks, base = [], 0
    for n in [A_VDIM] * A_HEADS + [HEAD_DIM] * B_KV_HEADS:
        blocks.append((base, n))
        base += n + ONES_ROWS
    return blocks, base


VT_BLOCKS, VT_ROWS = _vt_row_layout()


def _rope(z, cos, s_up, s_dn):
    n = z.shape[-1]
    return z * cos + pltpu.roll(z, n - ROPE_FREQS, 1) * s_up + pltpu.roll(z, ROPE_FREQS, 1) * s_dn


def _group_mean_sq(z, gsum_ref):
    z2 = z * z
    hi = z2.astype(BF16)
    lo = (z2 - hi.astype(F32)).astype(BF16)
    g = gsum_ref[...]
    ss = jnp.dot(hi, g, preferred_element_type=F32) + jnp.dot(lo, g, preferred_element_type=F32)
    return ss * (1.0 / HEAD_DIM)


def _proj_kernel(x_ref, mod_ref, w_ref, cos_ref, sup_ref, sdn_ref, qg_ref, kg_ref, gsum_ref,
                 q_ref, k_ref, vt_ref):
    h = (x_ref[...] * (1.0 + mod_ref[1:2, :]) + mod_ref[0:1, :]).astype(BF16)
    cos, s_up, s_dn = cos_ref[...], sup_ref[...], sdn_ref[...]
    scale = HEAD_DIM ** -0.5 * math.log2(math.e)

    def lane_groups(z, fn):
        return [fn(z[:, g * LANES:(g + 1) * LANES]) for g in range(z.shape[1] // LANES)]

    def rope_all(z, mul):
        return jnp.concatenate(lane_groups(z, lambda t: _rope(t, cos, s_up, s_dn) * mul), axis=1)

    def norm_all(z, gain_ref):
        gain = gain_ref[...]
        return jnp.concatenate(
            lane_groups(z, lambda t: t * lax.rsqrt(_group_mean_sq(t, gsum_ref) + EPS) * gain), axis=1)

    aq = jnp.dot(h, w_ref[:, 0:512], preferred_element_type=F32)
    q_ref[:, 0:512] = rope_all(aq, scale).astype(BF16)
    bq = jnp.dot(h, w_ref[:, 512:1024], preferred_element_type=F32)
    q_ref[:, 512:1024] = rope_all(norm_all(bq, qg_ref), scale).astype(BF16)
    ak = jnp.dot(h, w_ref[:, 1024:1536], preferred_element_type=F32)
    k_ref[:, 0:512] = rope_all(ak, 1.0).astype(BF16)
    bk = jnp.dot(h, w_ref[:, 1536:1664], preferred_element_type=F32)
    k_ref[:, 512:640] = rope_all(norm_all(bk, kg_ref), 1.0).astype(BF16)
    v = jnp.dot(h, w_ref[:, 1664:2304], preferred_element_type=F32)
    vt = v.T
    ones = jnp.ones((ONES_ROWS, TILE), F32)
    pieces, col = [], 0
    for _, n in VT_BLOCKS:
        pieces += [vt[col:col + n], ones]
        col += n
    vt_ref[...] = jnp.concatenate(pieces, axis=0).astype(BF16)


def _even_projection(xs, mod_l, w_in, cos_t, sup_t, sdn_t, qg, kg, gsum, n_batch):
    bsz, t_all, d = xs.shape
    nt = t_all // TILE
    return pl.pallas_call(
        _proj_kernel,
        grid=(bsz, nt),
        in_specs=[pl.BlockSpec((None, TILE, d), lambda b, i: (b, i, 0)),
                  pl.BlockSpec((None, N_MOD, d), _mod_row_map(n_batch, True)),
                  _const_spec(w_in.shape),
                  pl.BlockSpec((TILE, LANES), lambda b, i: (i, 0)),
                  pl.BlockSpec((TILE, LANES), lambda b, i: (i, 0)),
                  pl.BlockSpec((TILE, LANES), lambda b, i: (i, 0)),
                  _const_spec(qg.shape), _const_spec(kg.shape), _const_spec(gsum.shape)],
        out_specs=[pl.BlockSpec((None, TILE, Q_COLS), lambda b, i: (b, i, 0)),
                   pl.BlockSpec((None, TILE, K_COLS), lambda b, i: (b, i, 0)),
                   pl.BlockSpec((None, VT_ROWS, TILE), lambda b, i: (b, 0, i))],
        out_shape=[jax.ShapeDtypeStruct((bsz, t_all, Q_COLS), BF16),
                   jax.ShapeDtypeStruct((bsz, t_all, K_COLS), BF16),
                   jax.ShapeDtypeStruct((bsz, VT_ROWS, t_all), BF16)],
        compiler_params=_params("arbitrary", "arbitrary"),
    )(xs, mod_l, w_in, cos_t, sup_t, sdn_t, qg, kg, gsum)


def _key_chunk(n_keys):
    for parts in range(1, n_keys + 1):
        chunk, rem = divmod(n_keys, parts)
        if rem == 0 and chunk <= MAX_KEY_CHUNK and chunk % ONES_ROWS == 0:
            return chunk
    raise ValueError(n_keys)


def _attn_kernel(q_ref, k_ref, vt_ref, lamv_ref, subg_ref, o_ref, ot_ref, s0_ref, s1_ref, p_ref, *,
                 lam_init):
    s_refs = (s0_ref, s1_ref)
    n_keys = k_ref.shape[0]
    chunk = _key_chunk(n_keys)
    n_chunks = n_keys // chunk
    lower = lax.broadcasted_iota(jnp.int32, (TILE, LANES), 1) < HEAD_DIM
    zero = jnp.zeros((TILE, LANES), BF16)

    lv = lamv_ref[...]
    lam = (jnp.exp(jnp.sum(lv[0:1] * lv[1:2], axis=-1, keepdims=True))
           - jnp.exp(jnp.sum(lv[2:3] * lv[3:4], axis=-1, keepdims=True)) + lam_init)
    subg = subg_ref[...]

    q0 = A_HEADS * LANES
    pairs = [(slice(hd * LANES, (hd + 1) * LANES), slice(hd * LANES, (hd + 1) * LANES),
              VT_BLOCKS[hd], VT_BLOCKS[hd]) for hd in range(A_HEADS)]
    pairs += [(slice(q0 + j * LANES, q0 + (j + 1) * LANES), slice(q0, q0 + LANES),
               VT_BLOCKS[A_HEADS], VT_BLOCKS[A_HEADS + 1]) for j in range(B_HEADS // 2)]

    def masked_q(i):
        qt = q_ref[:, pairs[i][0]]
        return jnp.concatenate([jnp.where(lower, qt, zero), jnp.where(lower, zero, qt)], axis=0)

    def score_step(i, qcat, c, m8):
        rows = pl.ds(pl.multiple_of(c * chunk, ONES_ROWS), chunk)
        s = lax.dot_general(k_ref[rows, pairs[i][1]], qcat, (((1,), (1,)), ((), ())),
                            preferred_element_type=F32)
        s_refs[i % 2][rows, :] = s
        return jnp.maximum(m8, jnp.max(s.reshape(chunk // SUBLANES, SUBLANES, 2 * TILE), axis=0))

    def exp_step(i, c, m):
        rows = pl.ds(pl.multiple_of(c * chunk, ONES_ROWS), chunk)
        p_ref[rows, :] = jnp.exp2(s_refs[i % 2][rows, :] - m).astype(BF16)

    def value_step(i):
        _, _, blk_lo, blk_hi = pairs[i]
        outs = []
        if blk_lo == blk_hi:
            base, n = blk_lo
            acc = jnp.dot(vt_ref[base:base + n + ONES_ROWS, :], p_ref[...], preferred_element_type=F32)
            accs = [acc[:, :TILE], acc[:, TILE:]]
        else:
            accs = [jnp.dot(vt_ref[base:base + n + ONES_ROWS, :], p_ref[:, half * TILE:(half + 1) * TILE],
                            preferred_element_type=F32)
                    for half, (base, n) in enumerate((blk_lo, blk_hi))]
        for (_, n), acc in zip((blk_lo, blk_hi), accs):
            outs.append(acc[:n] * (1.0 / acc[n:n + 1]))
        return outs

    m8_init = jnp.full((SUBLANES, 2 * TILE), NEG_BIG, F32)
    qcat = masked_q(0)
    m8 = lax.fori_loop(0, n_chunks, lambda c, m8: score_step(0, qcat, c, m8), m8_init)
    for i in range(len(pairs)):
        m = jnp.max(m8, axis=0, keepdims=True)
        if i + 1 < len(pairs):
            qcat = masked_q(i + 1)

            def both(c, m8, i=i, m=m, qcat=qcat):
                m8 = score_step(i + 1, qcat, c, m8)
                exp_step(i, c, m)
                return m8

            m8 = lax.fori_loop(0, n_chunks, both, m8_init)
        else:
            def exp_only(c, carry, i=i, m=m):
                exp_step(i, c, m)
                return carry

            lax.fori_loop(0, n_chunks, exp_only, 0)
        o_lo, o_hi = value_step(i)
        if i < A_HEADS:
            o = o_lo - lam * o_hi
            ms = jnp.mean(o * o, axis=0, keepdims=True)
            ot_ref[i * A_VDIM:(i + 1) * A_VDIM, :] = o * lax.rsqrt(ms + EPS) * subg * (1.0 - lam_init)
        else:
            base = q0 + (i - A_HEADS) * LANES
            ot_ref[base:base + HEAD_DIM, :] = o_lo
            ot_ref[base + HEAD_DIM:base + LANES, :] = o_hi
    o_ref[...] = ot_ref[...].T.astype(BF16)


def _attention(q_all, k_all, vt_all, lam_vecs, subg, lam_init, context):
    bsz, t_all, _ = q_all.shape
    n_keys = CTX_LEN if context else t_all
    nq = 1 if context else (t_all - CTX_LEN) // TILE
    q_off = 0 if context else CTX_LEN // TILE
    kern = functools.partial(_attn_kernel, lam_init=lam_init)
    per_batch = dict(pipeline_mode=pl.Buffered(1))
    return pl.pallas_call(
        kern,
        grid=(bsz, nq),
        in_specs=[pl.BlockSpec((None, TILE, Q_COLS), lambda b, i: (b, i + q_off, 0)),
                  pl.BlockSpec((None, n_keys, K_COLS), lambda b, i: (b, 0, 0), **per_batch),
                  pl.BlockSpec((None, VT_ROWS, n_keys), lambda b, i: (b, 0, 0), **per_batch),
                  _const_spec(lam_vecs.shape), _const_spec(subg.shape)],
        out_specs=pl.BlockSpec((None, TILE, Q_COLS), lambda b, i: (b, i, 0)),
        out_shape=jax.ShapeDtypeStruct((bsz, nq * TILE, Q_COLS), BF16),
        scratch_shapes=[pltpu.VMEM((Q_COLS, TILE), F32),
                        pltpu.VMEM((n_keys, 2 * TILE), F32),
                        pltpu.VMEM((n_keys, 2 * TILE), F32),
                        pltpu.VMEM((n_keys, 2 * TILE), BF16)],
        compiler_params=_params("arbitrary", "arbitrary"),
    )(q_all, k_all, vt_all, lam_vecs, subg)


def _post_kernel(*refs, alpha, has_ctx):
    if has_ctx:
        o_ref, oc_ref, x_ref, mod_ref, ln_ref, wo_ref, w1_ref, w2_ref, out_ref = refs
        o = jnp.where(pl.program_id(1) == 0, oc_ref[...], o_ref[...])
    else:
        o_ref, x_ref, mod_ref, ln_ref, wo_ref, w1_ref, w2_ref, out_ref = refs
        o = o_ref[...]
    y = jnp.dot(o, wo_ref[...], preferred_element_type=F32)
    out_ref[...] = _post_mixer(x_ref[...], y, mod_ref, ln_ref, w1_ref, w2_ref, alpha)


def _even_post(o_lat, o_ctx, xs, mod_l, ln_l, w_out, w1, w2, n_batch, alpha):
    bsz, _, d = o_lat.shape
    has_ctx = o_ctx is not None
    off = CTX_LEN // TILE
    nt = xs.shape[1] // TILE - (0 if has_ctx else off)
    kern = functools.partial(_post_kernel, alpha=alpha, has_ctx=has_ctx)
    if has_ctx:
        o_specs = [pl.BlockSpec((None, TILE, d), lambda b, i: (b, jnp.maximum(i - off, 0), 0)),
                   pl.BlockSpec((None, TILE, d), lambda b, i: (b, 0, 0))]
        o_args = [o_lat, o_ctx]
        x_spec = pl.BlockSpec((None, TILE, d), lambda b, i: (b, i, 0))
    else:
        o_specs = [pl.BlockSpec((None, TILE, d), lambda b, i: (b, i, 0))]
        o_args = [o_lat]
        x_spec = pl.BlockSpec((None, TILE, d), lambda b, i: (b, i + off, 0))
    return pl.pallas_call(
        kern,
        grid=(bsz, nt),
        in_specs=o_specs + [x_spec,
                            pl.BlockSpec((None, N_MOD, d), _mod_row_map(n_batch, has_ctx)),
                            _const_spec(ln_l.shape), _const_spec(w_out.shape),
                            _const_spec(w1.shape), _const_spec(w2.shape)],
        out_specs=pl.BlockSpec((None, TILE, d), lambda b, i: (b, i, 0)),
        out_shape=jax.ShapeDtypeStruct((bsz, nt * TILE, d), F32),
        compiler_params=_params("arbitrary", "arbitrary"),
    )(*o_args, xs, mod_l, ln_l, w_out, w1, w2)


def _odd_kernel(x_ref, xp_ref, xn_ref, mod_ref, ln_ref, wi_ref, cw_ref, cb_ref, wo_ref, w1_ref, w2_ref,
                out_ref, u_ref, *, alpha, has_ctx, n_tiles):
    i = pl.program_id(1)
    if has_ctx:
        first = i <= 1
        last = jnp.logical_or(i == 0, i == n_tiles - 1)
    else:
        first = i == 0
        last = i == n_tiles - 1
    x = x_ref[...]
    shift, scale1 = mod_ref[0:1, :], 1.0 + mod_ref[1:2, :]
    x_ext = jnp.concatenate([xp_ref[...], x, xn_ref[...]], axis=0)
    h_ext = (x_ext * scale1 + shift).astype(BF16)
    z_c = jnp.dot(h_ext, wi_ref[:, D_MODEL:2 * D_MODEL], preferred_element_type=F32)
    z_x = jnp.dot(h_ext, wi_ref[:, 2 * D_MODEL:3 * D_MODEL], preferred_element_type=F32)
    row = lax.broadcasted_iota(jnp.int32, (TILE + 2 * HALO, 1), 0)
    live_from = jnp.where(first, HALO, 0)
    live_to = jnp.where(last, HALO + TILE, TILE + 2 * HALO)
    live = jnp.logical_and(row >= live_from, row < live_to)
    u_ref[...] = jnp.where(live, z_c * z_x, 0.0)
    conv = (u_ref[HALO - 1:HALO - 1 + TILE, :] * cw_ref[0:1, :]
            + u_ref[HALO:HALO + TILE, :] * cw_ref[1:2, :]
            + u_ref[HALO + 1:HALO + 1 + TILE, :] * cw_ref[2:3, :] + cb_ref[...])
    h = (x * scale1 + shift).astype(BF16)
    z_b = jnp.dot(h, wi_ref[:, 0:D_MODEL], preferred_element_type=F32)
    y = jnp.dot((z_b * conv).astype(BF16), wo_ref[...], preferred_element_type=F32)
    out_ref[...] = _post_mixer(x, y, mod_ref, ln_ref, w1_ref, w2_ref, alpha)


def _odd_layer(xs, mod_l, ln_l, w_in, conv_w, conv_b, w_out, w1, w2, has_ctx, n_batch, alpha):
    bsz, t_all, d = xs.shape
    nt = t_all // TILE
    per_tile = TILE // HALO
    n_halo_blocks = t_all // HALO
    kern = functools.partial(_odd_kernel, alpha=alpha, has_ctx=has_ctx, n_tiles=nt)
    return pl.pallas_call(
        kern,
        grid=(bsz, nt),
        in_specs=[pl.BlockSpec((None, TILE, d), lambda b, i: (b, i, 0)),
                  pl.BlockSpec((None, HALO, d), lambda b, i: (b, jnp.maximum(i * per_tile - 1, 0), 0)),
                  pl.BlockSpec((None, HALO, d),
                               lambda b, i: (b, jnp.minimum((i + 1) * per_tile, n_halo_blocks - 1), 0)),
                  pl.BlockSpec((None, N_MOD, d), _mod_row_map(n_batch, has_ctx)),
                  _const_spec(ln_l.shape), _const_spec(w_in.shape), _const_spec(conv_w.shape),
                  _const_spec(conv_b.shape), _const_spec(w_out.shape),
                  _const_spec(w1.shape), _const_spec(w2.shape)],
        out_specs=pl.BlockSpec((None, TILE, d), lambda b, i: (b, i, 0)),
        out_shape=jax.ShapeDtypeStruct((bsz, t_all, d), F32),
        scratch_shapes=[pltpu.VMEM((TILE + 2 * HALO, d), F32)],
        compiler_params=_params("arbitrary", "arbitrary"),
    )(xs, xs, xs, mod_l, ln_l, w_in, conv_w, conv_b, w_out, w1, w2)


def _rope_tables(n_lat):
    t = jnp.arange(n_lat)
    pos = jnp.stack([(t // GRID_W).astype(F32), (t % GRID_W).astype(F32)], axis=-1)
    inv_freq = ROPE_THETA ** (-jnp.arange(ROPE_FREQS, dtype=F32) / ROPE_FREQS)
    ang = pos[:, :, None] * inv_freq
    cos = jnp.repeat(jnp.cos(ang), 2, axis=1).reshape(n_lat, HEAD_DIM)
    sin = jnp.repeat(jnp.sin(ang), 2, axis=1).reshape(n_lat, HEAD_DIM)
    first_half = (np.arange(HEAD_DIM) % (2 * ROPE_FREQS)) < ROPE_FREQS
    s_up = jnp.where(first_half, -sin, 0.0)
    s_dn = jnp.where(first_half, 0.0, sin)
    reps = LANES // HEAD_DIM
    pad = lambda a, fill: jnp.concatenate(
        [jnp.full((CTX_LEN, LANES), fill, F32), jnp.tile(a, (1, reps))], axis=0)
    return pad(cos, 1.0), pad(s_up, 0.0), pad(s_dn, 0.0)


def _gqa_head_order():
    half = B_HEADS // 2
    return [h for j in range(half) for h in (j, j + half)]


def kernel(x, c, ctx, c_ctx, ada_w, ada_b, attn_w_in, attn_w_out, diff_lambda, diff_subln_g, q_norm_g, k_norm_g, conv_w_in, conv_w, conv_b, conv_w_out, mlp_w1, mlp_w2, ln_g, ln_b):
    bsz, n_lat, d = x.shape
    depth = ada_w.shape[0]
    assert d == D_MODEL and ctx.shape[1] == CTX_LEN == TILE and n_lat % TILE == 0
    alpha = (2 * depth) ** 0.25

    n_rows = -(-(bsz + 1) // SUBLANES) * SUBLANES
    c_all = jnp.concatenate([c, c_ctx[None, :], jnp.zeros((n_rows - bsz - 1, d), F32)], axis=0)
    mod = _ada_modulation(c_all, ada_w, ada_b).reshape(depth, n_rows, N_MOD, d)

    cos_t, sup_t, sdn_t = _rope_tables(n_lat)
    order = _gqa_head_order()
    bq_cols = np.concatenate([1536 + HEAD_DIM * h + np.arange(HEAD_DIM) for h in order])
    in_cols = np.concatenate([np.arange(0, 512), bq_cols, np.arange(512, 1024), np.arange(2048, 2176),
                              np.arange(1024, 1536), np.arange(2176, 2304)])
    out_rows = np.concatenate([np.arange(0, 512)] + [512 + HEAD_DIM * h + np.arange(HEAD_DIM) for h in order])
    gsum = jnp.asarray(np.kron(np.eye(LANES // HEAD_DIM), np.ones((HEAD_DIM, HEAD_DIM))), BF16)
    ln = jnp.stack([ln_g[:, 0], ln_b[:, 0], ln_g[:, 1], ln_b[:, 1]], axis=1)

    xs = jnp.concatenate([ctx, x], axis=1)
    has_ctx = True
    for l in range(depth):
        ctx_out = any(j % 2 == 0 for j in range(l + 1, depth))
        w1 = mlp_w1[l].astype(BF16)
        w2 = mlp_w2[l].astype(BF16)
        if l % 2 == 0:
            e = l // 2
            assert has_ctx
            lam_init = 0.8 - 0.6 * math.exp(-0.3 * l)
            w_in = attn_w_in[e][:, in_cols].astype(BF16)
            w_out = attn_w_out[e][out_rows, :].astype(BF16)
            tile2 = lambda g: jnp.tile(g, LANES // HEAD_DIM)[None, :]
            q_all, k_all, vt_all = _even_projection(xs, mod[l], w_in, cos_t, sup_t, sdn_t,
                                                    tile2(q_norm_g[e]), tile2(k_norm_g[e]), gsum, bsz)
            attend = functools.partial(_attention, q_all, k_all, vt_all, diff_lambda[e],
                                       diff_subln_g[e][:, None], lam_init)
            o_lat = attend(False)
            o_ctx = attend(True) if ctx_out else None
            xs = _even_post(o_lat, o_ctx, xs, mod[l], ln[l], w_out, w1, w2, bsz, alpha)
            has_ctx = ctx_out
        else:
            od = l // 2
            if has_ctx and not ctx_out:
                xs = xs[:, CTX_LEN:]
                has_ctx = False
            xs = _odd_layer(xs, mod[l], ln[l], conv_w_in[od].astype(BF16), conv_w[od], conv_b[od][None, :],
                            conv_w_out[od].astype(BF16), w1, w2, has_ctx, bsz, alpha)
    return xs[:, CTX_LEN:] if has_ctx else xs
```

```python
import functools
import math

import numpy as np
import jax
import jax.numpy as jnp
from jax import lax
from jax.experimental import pallas as pl
from jax.experimental.pallas import tpu as pltpu

F32 = jnp.float32
BF16 = jnp.bfloat16

D_MODEL = 1024
CTX_LEN = 256
GRID_W = 64
HEAD_DIM = 64
A_HEADS = 4
A_VDIM = 2 * HEAD_DIM
B_HEADS = 8
B_KV_HEADS = 2
Q_COLS = 1024
K_COLS = 640
V_COLS = 640
ONES_ROWS = 16
MLP_HIDDEN = 4 * D_MODEL
ROPE_THETA = 10000.0
ROPE_FREQS = HEAD_DIM // 4
N_MOD = 6
EPS = 1e-6

TILE = 256
LANES = 128
SUBLANES = 8
HALO = SUBLANES
HID_CHUNK = 1024
MAX_KEY_CHUNK = 4352
SCORE_ROWS = 544
EXP_ROWS = 128
NEG_BIG = -1e30
VMEM_LIMIT = 56 * 1024 * 1024


def _const_spec(shape):
    nd = len(shape)
    return pl.BlockSpec(shape, lambda *_: (0,) * nd, pipeline_mode=pl.Buffered(1))


def _layer_spec(stacked, l):
    nd = stacked.ndim
    return pl.BlockSpec((None,) + stacked.shape[1:], lambda *_: (l,) + (0,) * (nd - 1),
                        pipeline_mode=pl.Buffered(1))


def _stream_specs(xs, off=0):
    if isinstance(xs, tuple):
        d = xs[0].shape[-1]
        n_ctx = CTX_LEN // TILE
        return ([pl.BlockSpec((None, TILE, d), lambda b, i: (b, 0, 0)),
                 pl.BlockSpec((None, TILE, d), lambda b, i: (b, jnp.maximum(i - n_ctx, 0), 0))], list(xs))
    return [pl.BlockSpec((None, TILE, xs.shape[-1]), lambda b, i: (b, i + off, 0))], [xs]


def _read_stream(refs):
    if len(refs) == 2:
        return jnp.where(pl.program_id(1) == 0, refs[0][...], refs[1][...])
    return refs[0][...]


def _params(*sem):
    return pltpu.CompilerParams(dimension_semantics=sem, vmem_limit_bytes=VMEM_LIMIT)


def _ada_kernel(c_ref, w_ref, b_ref, o_ref):
    c = c_ref[...]
    act = c * (1.0 / (1.0 + jnp.exp(-c)))
    o_ref[...] = jnp.dot(act.astype(BF16), w_ref[...].astype(BF16),
                         preferred_element_type=F32) + b_ref[...]


def _ada_modulation(c_all, ada_w, ada_b):
    depth, d, n = ada_w.shape
    rows = c_all.shape[0]
    tn = 1536
    return pl.pallas_call(
        _ada_kernel,
        grid=(depth, n // tn),
        in_specs=[pl.BlockSpec((rows, d), lambda l, j: (0, 0)),
                  pl.BlockSpec((None, d, tn), lambda l, j: (l, 0, j)),
                  pl.BlockSpec((None, 1, tn), lambda l, j: (l, 0, j))],
        out_specs=pl.BlockSpec((None, rows, tn), lambda l, j: (l, 0, j)),
        out_shape=jax.ShapeDtypeStruct((depth, rows, n), F32),
        compiler_params=_params("arbitrary", "arbitrary"),
    )(c_all, ada_w, ada_b.reshape(depth, 1, n))


def _layer_norm(v, g, b):
    mu = jnp.mean(v, axis=-1, keepdims=True)
    d = v - mu
    var = jnp.mean(d * d, axis=-1, keepdims=True)
    return d * lax.rsqrt(var + EPS) * g + b


def _post_mixer(x, y, mod_ref, ln_ref, w1_ref, w2_ref, alpha):
    x1 = _layer_norm(alpha * x + mod_ref[2:3, :] * y, ln_ref[0:1, :], ln_ref[1:2, :])
    h2 = (x1 * (1.0 + mod_ref[4:5, :]) + mod_ref[3:4, :]).astype(BF16)
    y2 = jnp.zeros_like(x1)
    for c in range(MLP_HIDDEN // HID_CHUNK):
        sl = slice(c * HID_CHUNK, (c + 1) * HID_CHUNK)
        hid = jnp.maximum(jnp.dot(h2, w1_ref[:, sl], preferred_element_type=F32), 0.0)
        y2 = y2 + jnp.dot((hid * hid).astype(BF16), w2_ref[sl, :], preferred_element_type=F32)
    return _layer_norm(alpha * x1 + mod_ref[5:6, :] * y2, ln_ref[2:3, :], ln_ref[3:4, :])


def _mod_row_map(l, n_batch, has_ctx):
    if has_ctx:
        return lambda b, i: (l, jnp.where(i == 0, n_batch, b), 0, 0)
    return lambda b, i: (l, b, 0, 0)


def _vt_row_layout():
    blocks, base = [], 0
    for n in [A_VDIM] * A_HEADS + [HEAD_DIM] * B_KV_HEADS:
        blocks.append((base, n))
        base += n + ONES_ROWS
    return blocks, base


VT_BLOCKS, VT_ROWS = _vt_row_layout()


def _rope(z, cos, s_up, s_dn):
    n = z.shape[-1]
    return z * cos + pltpu.roll(z, n - ROPE_FREQS, 1) * s_up + pltpu.roll(z, ROPE_FREQS, 1) * s_dn


def _group_mean_sq(z, gsum_ref):
    z2 = z * z
    hi = z2.astype(BF16)
    lo = (z2 - hi.astype(F32)).astype(BF16)
    g = gsum_ref[...]
    ss = jnp.dot(hi, g, preferred_element_type=F32) + jnp.dot(lo, g, preferred_element_type=F32)
    return ss * (1.0 / HEAD_DIM)


def _proj_kernel(*refs):
    (mod_ref, w_ref, cos_ref, sup_ref, sdn_ref, qg_ref, kg_ref, gsum_ref, q_ref, k_ref, vt_ref) = refs[-11:]
    h = (_read_stream(refs[:-11]) * (1.0 + mod_ref[1:2, :]) + mod_ref[0:1, :]).astype(BF16)
    cos, s_up, s_dn = cos_ref[...], sup_ref[...], sdn_ref[...]
    scale = HEAD_DIM ** -0.5 * math.log2(math.e)

    def lane_groups(z, fn):
        return [fn(z[:, g * LANES:(g + 1) * LANES]) for g in range(z.shape[1] // LANES)]

    def rope_all(z, mul):
        return jnp.concatenate(lane_groups(z, lambda t: _rope(t, cos, s_up, s_dn) * mul), axis=1)

    def norm_all(z, gain_ref):
        gain = gain_ref[...]
        return jnp.concatenate(
            lane_groups(z, lambda t: t * lax.rsqrt(_group_mean_sq(t, gsum_ref) + EPS) * gain), axis=1)

    aq = jnp.dot(h, w_ref[:, 0:512], preferred_element_type=F32)
    q_ref[:, 0:512] = rope_all(aq, scale).astype(BF16)
    bq = jnp.dot(h, w_ref[:, 512:1024], preferred_element_type=F32)
    q_ref[:, 512:1024] = rope_all(norm_all(bq, qg_ref), scale).astype(BF16)
    ak = jnp.dot(h, w_ref[:, 1024:1536], preferred_element_type=F32)
    k_ref[:, 0:512] = rope_all(ak, 1.0).astype(BF16)
    bk = jnp.dot(h, w_ref[:, 1536:1664], preferred_element_type=F32)
    k_ref[:, 512:640] = rope_all(norm_all(bk, kg_ref), 1.0).astype(BF16)
    v = jnp.dot(h, w_ref[:, 1664:2304], preferred_element_type=F32)
    vt = v.T
    ones = jnp.ones((ONES_ROWS, TILE), F32)
    pieces, col = [], 0
    for _, n in VT_BLOCKS:
        pieces += [vt[col:col + n], ones]
        col += n
    vt_ref[...] = jnp.concatenate(pieces, axis=0).astype(BF16)


def _even_projection(xs, mod, l, w_in, e, cos_t, sup_t, sdn_t, qg, kg, gsum, n_batch):
    x_specs, x_args = _stream_specs(xs)
    bsz, d = x_args[0].shape[0], x_args[0].shape[-1]
    t_all = sum(a.shape[1] for a in x_args)
    nt = t_all // TILE
    return pl.pallas_call(
        _proj_kernel,
        grid=(bsz, nt),
        in_specs=x_specs + [
                  pl.BlockSpec((None, None, N_MOD, d), _mod_row_map(l, n_batch, True)),
                  _layer_spec(w_in, e),
                  pl.BlockSpec((TILE, LANES), lambda b, i: (i, 0)),
                  pl.BlockSpec((TILE, LANES), lambda b, i: (i, 0)),
                  pl.BlockSpec((TILE, LANES), lambda b, i: (i, 0)),
                  _const_spec(qg.shape), _const_spec(kg.shape), _const_spec(gsum.shape)],
        out_specs=[pl.BlockSpec((None, TILE, Q_COLS), lambda b, i: (b, i, 0)),
                   pl.BlockSpec((None, TILE, K_COLS), lambda b, i: (b, i, 0)),
                   pl.BlockSpec((None, VT_ROWS, TILE), lambda b, i: (b, 0, i))],
        out_shape=[jax.ShapeDtypeStruct((bsz, t_all, Q_COLS), BF16),
                   jax.ShapeDtypeStruct((bsz, t_all, K_COLS), BF16),
                   jax.ShapeDtypeStruct((bsz, VT_ROWS, t_all), BF16)],
        compiler_params=_params("arbitrary", "arbitrary"),
    )(*x_args, mod, w_in, cos_t, sup_t, sdn_t, qg, kg, gsum)


def _key_chunk(n_keys, cap, multiple):
    for parts in range(1, n_keys + 1):
        chunk, rem = divmod(n_keys, parts)
        if rem == 0 and chunk <= cap and chunk % multiple == 0:
            return chunk
    raise ValueError(n_keys)


def _attn_kernel(q_ref, k_ref, vt_ref, lamv_ref, subg_ref, o_ref, ot_ref, s0_ref, s1_ref, p_ref, *,
                 lam_init):
    s_refs = (s0_ref, s1_ref)
    n_keys = k_ref.shape[0]
    chunk = _key_chunk(n_keys, MAX_KEY_CHUNK, EXP_ROWS)
    n_chunks = n_keys // chunk
    lower = lax.broadcasted_iota(jnp.int32, (TILE, LANES), 1) < HEAD_DIM
    zero = jnp.zeros((TILE, LANES), BF16)

    lv = lamv_ref[...]
    lam = (jnp.exp(jnp.sum(lv[0:1] * lv[1:2], axis=-1, keepdims=True))
           - jnp.exp(jnp.sum(lv[2:3] * lv[3:4], axis=-1, keepdims=True)) + lam_init)
    subg = subg_ref[...]

    q0 = A_HEADS * LANES
    pairs = [(slice(hd * LANES, (hd + 1) * LANES), slice(hd * LANES, (hd + 1) * LANES),
              VT_BLOCKS[hd], VT_BLOCKS[hd]) for hd in range(A_HEADS)]
    pairs += [(slice(q0 + j * LANES, q0 + (j + 1) * LANES), slice(q0, q0 + LANES),
               VT_BLOCKS[A_HEADS], VT_BLOCKS[A_HEADS + 1]) for j in range(B_HEADS // 2)]

    def masked_q(i):
        qt = q_ref[:, pairs[i][0]]
        return jnp.concatenate([jnp.where(lower, qt, zero), jnp.where(lower, zero, qt)], axis=0)

    def score_step(i, qcat, c, m8):
        sub = _key_chunk(chunk, SCORE_ROWS, ONES_ROWS)
        for j in range(chunk // sub):
            rows = pl.ds(pl.multiple_of(c * chunk + j * sub, ONES_ROWS), sub)
            s = lax.dot_general(k_ref[rows, pairs[i][1]], qcat, (((1,), (1,)), ((), ())),
                                preferred_element_type=F32)
            s_refs[i % 2][rows, :] = s
            m8 = jnp.maximum(m8, jnp.max(s.reshape(sub // SUBLANES, SUBLANES, 2 * TILE), axis=0))
        return m8

    def exp_step(i, c, m):
        for j in range(chunk // EXP_ROWS):
            rows = pl.ds(pl.multiple_of(c * chunk + j * EXP_ROWS, EXP_ROWS), EXP_ROWS)
            p_ref[rows, :] = jnp.exp2(s_refs[i % 2][rows, :] - m).astype(BF16)

    def value_step(i):
        _, _, blk_lo, blk_hi = pairs[i]
        outs = []
        if blk_lo == blk_hi:
            base, n = blk_lo
            acc = jnp.dot(vt_ref[base:base + n + ONES_ROWS, :], p_ref[...], preferred_element_type=F32)
            accs = [acc[:, :TILE], acc[:, TILE:]]
        else:
            accs = [jnp.dot(vt_ref[base:base + n + ONES_ROWS, :], p_ref[:, half * TILE:(half + 1) * TILE],
                            preferred_element_type=F32)
                    for half, (base, n) in enumerate((blk_lo, blk_hi))]
        for (_, n), acc in zip((blk_lo, blk_hi), accs):
            outs.append(acc[:n] * (1.0 / acc[n:n + 1]))
        return outs

    m8_init = jnp.full((SUBLANES, 2 * TILE), NEG_BIG, F32)
    qcat = masked_q(0)
    m8 = lax.fori_loop(0, n_chunks, lambda c, m8: score_step(0, qcat, c, m8), m8_init)
    for i in range(len(pairs)):
        m = jnp.max(m8, axis=0, keepdims=True)
        if i + 1 < len(pairs):
            qcat = masked_q(i + 1)

            def both(c, m8, i=i, m=m, qcat=qcat):
                m8 = score_step(i + 1, qcat, c, m8)
                exp_step(i, c, m)
                return m8

            m8 = lax.fori_loop(0, n_chunks, both, m8_init)
        else:
            def exp_only(c, carry, i=i, m=m):
                exp_step(i, c, m)
                return carry

            lax.fori_loop(0, n_chunks, exp_only, 0)
        o_lo, o_hi = value_step(i)
        if i < A_HEADS:
            o = o_lo - lam * o_hi
            ms = jnp.mean(o * o, axis=0, keepdims=True)
            ot_ref[i * A_VDIM:(i + 1) * A_VDIM, :] = o * lax.rsqrt(ms + EPS) * subg * (1.0 - lam_init)
        else:
            base = q0 + (i - A_HEADS) * LANES
            ot_ref[base:base + HEAD_DIM, :] = o_lo
            ot_ref[base + HEAD_DIM:base + LANES, :] = o_hi
    o_ref[...] = ot_ref[...].T.astype(BF16)


def _attention(q_all, k_all, vt_all, lam_vecs, subg, lam_init, context):
    bsz, t_all, _ = q_all.shape
    n_keys = CTX_LEN if context else t_all
    nq = 1 if context else (t_all - CTX_LEN) // TILE
    q_off = 0 if context else CTX_LEN // TILE
    kern = functools.partial(_attn_kernel, lam_init=lam_init)
    per_batch = dict(pipeline_mode=pl.Buffered(1))
    return pl.pallas_call(
        kern,
        grid=(bsz, nq),
        in_specs=[pl.BlockSpec((None, TILE, Q_COLS), lambda b, i: (b, i + q_off, 0)),
                  pl.BlockSpec((None, n_keys, K_COLS), lambda b, i: (b, 0, 0), **per_batch),
                  pl.BlockSpec((None, VT_ROWS, n_keys), lambda b, i: (b, 0, 0), **per_batch),
                  _const_spec(lam_vecs.shape), _const_spec(subg.shape)],
        out_specs=pl.BlockSpec((None, TILE, Q_COLS), lambda b, i: (b, i, 0)),
        out_shape=jax.ShapeDtypeStruct((bsz, nq * TILE, Q_COLS), BF16),
        scratch_shapes=[pltpu.VMEM((Q_COLS, TILE), F32),
                        pltpu.VMEM((n_keys, 2 * TILE), F32),
                        pltpu.VMEM((n_keys, 2 * TILE), F32),
                        pltpu.VMEM((n_keys, 2 * TILE), BF16)],
        compiler_params=_params("arbitrary", "arbitrary"),
    )(q_all, k_all, vt_all, lam_vecs, subg)


def _post_kernel(*refs, alpha, n_o):
    mod_ref, ln_ref, wo_ref, w1_ref, w2_ref, out_ref = refs[-6:]
    o = _read_stream(refs[:n_o])
    x = _read_stream(refs[n_o:-6])
    y = jnp.dot(o, wo_ref[...], preferred_element_type=F32)
    out_ref[...] = _post_mixer(x, y, mod_ref, ln_ref, w1_ref, w2_ref, alpha)


def _even_post(o_lat, o_ctx, xs, mod, ln, l, w_out, e, w1, w2, n_batch, alpha):
    bsz, _, d = o_lat.shape
    has_ctx = o_ctx is not None
    off = CTX_LEN // TILE
    if has_ctx:
        o_specs, o_args = _stream_specs((o_ctx, o_lat))
        x_specs, x_args = _stream_specs(xs)
    else:
        o_specs, o_args = _stream_specs(o_lat)
        x_specs, x_args = _stream_specs(xs[1]) if isinstance(xs, tuple) else _stream_specs(xs, off)
    nt = sum(a.shape[1] for a in o_args) // TILE
    kern = functools.partial(_post_kernel, alpha=alpha, n_o=len(o_args))
    return pl.pallas_call(
        kern,
        grid=(bsz, nt),
        in_specs=o_specs + x_specs + [
            pl.BlockSpec((None, None, N_MOD, d), _mod_row_map(l, n_batch, has_ctx)),
            _layer_spec(ln, l), _layer_spec(w_out, e), _layer_spec(w1, l), _layer_spec(w2, l)],
        out_specs=pl.BlockSpec((None, TILE, d), lambda b, i: (b, i, 0)),
        out_shape=jax.ShapeDtypeStruct((bsz, nt * TILE, d), F32),
        compiler_params=_params("arbitrary", "arbitrary"),
    )(*o_args, *x_args, mod, ln, w_out, w1, w2)


def _odd_kernel(x_ref, xp_ref, xn_ref, mod_ref, ln_ref, wi_ref, cw_ref, cb_ref, wo_ref, w1_ref, w2_ref,
                out_ref, u_ref, *, alpha, has_ctx, n_tiles):
    i = pl.program_id(1)
    if has_ctx:
        first = i <= 1
        last = jnp.logical_or(i == 0, i == n_tiles - 1)
    else:
        first = i == 0
        last = i == n_tiles - 1
    x = x_ref[...]
    shift, scale1 = mod_ref[0:1, :], 1.0 + mod_ref[1:2, :]
    x_ext = jnp.concatenate([xp_ref[...], x, xn_ref[...]], axis=0)
    h_ext = (x_ext * scale1 + shift).astype(BF16)
    z_c = jnp.dot(h_ext, wi_ref[:, D_MODEL:2 * D_MODEL], preferred_element_type=F32)
    z_x = jnp.dot(h_ext, wi_ref[:, 2 * D_MODEL:3 * D_MODEL], preferred_element_type=F32)
    row = lax.broadcasted_iota(jnp.int32, (TILE + 2 * HALO, 1), 0)
    live_from = jnp.where(first, HALO, 0)
    live_to = jnp.where(last, HALO + TILE, TILE + 2 * HALO)
    live = jnp.logical_and(row >= live_from, row < live_to)
    u_ref[...] = jnp.where(live, z_c * z_x, 0.0)
    conv = (u_ref[HALO - 1:HALO - 1 + TILE, :] * cw_ref[0:1, :]
            + u_ref[HALO:HALO + TILE, :] * cw_ref[1:2, :]
            + u_ref[HALO + 1:HALO + 1 + TILE, :] * cw_ref[2:3, :] + cb_ref[...])
    h = (x * scale1 + shift).astype(BF16)
    z_b = jnp.dot(h, wi_ref[:, 0:D_MODEL], preferred_element_type=F32)
    y = jnp.dot((z_b * conv).astype(BF16), wo_ref[...], preferred_element_type=F32)
    out_ref[...] = _post_mixer(x, y, mod_ref, ln_ref, w1_ref, w2_ref, alpha)


def _odd_layer(xs, mod, ln, l, w_in, conv_w, conv_b, w_out, od, w1, w2, has_ctx, n_batch, alpha):
    bsz, t_all, d = xs.shape
    nt = t_all // TILE
    per_tile = TILE // HALO
    n_halo_blocks = t_all // HALO
    kern = functools.partial(_odd_kernel, alpha=alpha, has_ctx=has_ctx, n_tiles=nt)
    return pl.pallas_call(
        kern,
        grid=(bsz, nt),
        in_specs=[pl.BlockSpec((None, TILE, d), lambda b, i: (b, i, 0)),
                  pl.BlockSpec((None, HALO, d), lambda b, i: (b, jnp.maximum(i * per_tile - 1, 0), 0)),
                  pl.BlockSpec((None, HALO, d),
                               lambda b, i: (b, jnp.minimum((i + 1) * per_tile, n_halo_blocks - 1), 0)),
                  pl.BlockSpec((None, None, N_MOD, d), _mod_row_map(l, n_batch, has_ctx)),
                  _layer_spec(ln, l), _layer_spec(w_in, od), _layer_spec(conv_w, od),
                  _layer_spec(conv_b, od), _layer_spec(w_out, od),
                  _layer_spec(w1, l), _layer_spec(w2, l)],
        out_specs=pl.BlockSpec((None, TILE, d), lambda b, i: (b, i, 0)),
        out_shape=jax.ShapeDtypeStruct((bsz, t_all, d), F32),
        scratch_shapes=[pltpu.VMEM((TILE + 2 * HALO, d), F32)],
        compiler_params=_params("arbitrary", "arbitrary"),
    )(xs, xs, xs, mod, ln, w_in, conv_w, conv_b, w_out, w1, w2)


def _rope_tables(n_lat):
    t = jnp.arange(n_lat)
    pos = jnp.stack([(t // GRID_W).astype(F32), (t % GRID_W).astype(F32)], axis=-1)
    inv_freq = ROPE_THETA ** (-jnp.arange(ROPE_FREQS, dtype=F32) / ROPE_FREQS)
    ang = pos[:, :, None] * inv_freq
    cos = jnp.repeat(jnp.cos(ang), 2, axis=1).reshape(n_lat, HEAD_DIM)
    sin = jnp.repeat(jnp.sin(ang), 2, axis=1).reshape(n_lat, HEAD_DIM)
    first_half = (np.arange(HEAD_DIM) % (2 * ROPE_FREQS)) < ROPE_FREQS
    s_up = jnp.where(first_half, -sin, 0.0)
    s_dn = jnp.where(first_half, 0.0, sin)
    reps = LANES // HEAD_DIM
    pad = lambda a, fill: jnp.concatenate(
        [jnp.full((CTX_LEN, LANES), fill, F32), jnp.tile(a, (1, reps))], axis=0)
    return pad(cos, 1.0), pad(s_up, 0.0), pad(s_dn, 0.0)


def _gqa_head_order():
    half = B_HEADS // 2
    return [h for j in range(half) for h in (j, j + half)]


def kernel(x, c, ctx, c_ctx, ada_w, ada_b, attn_w_in, attn_w_out, diff_lambda, diff_subln_g, q_norm_g, k_norm_g, conv_w_in, conv_w, conv_b, conv_w_out, mlp_w1, mlp_w2, ln_g, ln_b):
    bsz, n_lat, d = x.shape
    depth = ada_w.shape[0]
    assert d == D_MODEL and ctx.shape[1] == CTX_LEN == TILE and n_lat % TILE == 0
    alpha = (2 * depth) ** 0.25

    n_rows = -(-(bsz + 1) // SUBLANES) * SUBLANES
    c_all = jnp.concatenate([c, c_ctx[None, :], jnp.zeros((n_rows - bsz - 1, d), F32)], axis=0)
    mod = _ada_modulation(c_all, ada_w, ada_b).reshape(depth, n_rows, N_MOD, d)

    cos_t, sup_t, sdn_t = _rope_tables(n_lat)
    order = _gqa_head_order()
    bq_cols = np.concatenate([1536 + HEAD_DIM * h + np.arange(HEAD_DIM) for h in order])
    in_cols = np.concatenate([np.arange(0, 512), bq_cols, np.arange(512, 1024), np.arange(2048, 2176),
                              np.arange(1024, 1536), np.arange(2176, 2304)])
    out_rows = np.concatenate([np.arange(0, 512)] + [512 + HEAD_DIM * h + np.arange(HEAD_DIM) for h in order])
    gsum = jnp.asarray(np.kron(np.eye(LANES // HEAD_DIM), np.ones((HEAD_DIM, HEAD_DIM))), BF16)
    ln = jnp.stack([ln_g[:, 0], ln_b[:, 0], ln_g[:, 1], ln_b[:, 1]], axis=1)

    w1_all, w2_all = mlp_w1.astype(BF16), mlp_w2.astype(BF16)
    attn_in = attn_w_in[:, :, in_cols].astype(BF16)
    attn_out = attn_w_out[:, out_rows, :].astype(BF16)
    conv_in, conv_out = conv_w_in.astype(BF16), conv_w_out.astype(BF16)
    conv_bias = conv_b[:, None, :]
    tile2 = lambda g: jnp.tile(g, LANES // HEAD_DIM)[None, :]

    xs = (ctx, x)
    has_ctx = True
    for l in range(depth):
        ctx_out = any(j % 2 == 0 for j in range(l + 1, depth))
        if l % 2 == 0:
            e = l // 2
            assert has_ctx
            lam_init = 0.8 - 0.6 * math.exp(-0.3 * l)
            q_all, k_all, vt_all = _even_projection(xs, mod, l, attn_in, e, cos_t, sup_t, sdn_t,
                                                    tile2(q_norm_g[e]), tile2(k_norm_g[e]), gsum, bsz)
            attend = functools.partial(_attention, q_all, k_all, vt_all, diff_lambda[e],
                                       diff_subln_g[e][:, None], lam_init)
            o_lat = attend(False)
            o_ctx = attend(True) if ctx_out else None
            xs = _even_post(o_lat, o_ctx, xs, mod, ln, l, attn_out, e, w1_all, w2_all, bsz, alpha)
            has_ctx = ctx_out
        else:
            od = l // 2
            if isinstance(xs, tuple):
                xs = jnp.concatenate(xs, axis=1) if ctx_out else xs[1]
                has_ctx = ctx_out
            elif has_ctx and not ctx_out:
                xs = xs[:, CTX_LEN:]
                has_ctx = False
            xs = _odd_layer(xs, mod, ln, l, conv_in, conv_w, conv_bias, conv_out, od, w1_all, w2_all,
                            has_ctx, bsz, alpha)
    if isinstance(xs, tuple):
        return xs[1]
    return xs[:, CTX_LEN:] if has_ctx else xs
```

```python
import functools
import math

import numpy as np
import jax
import jax.numpy as jnp
from jax import lax
from jax.experimental import pallas as pl
from jax.experimental.pallas import tpu as pltpu

F32 = jnp.float32
BF16 = jnp.bfloat16

D_MODEL = 1024
CTX_LEN = 256
GRID_W = 64
HEAD_DIM = 64
A_HEADS = 4
A_VDIM = 2 * HEAD_DIM
B_HEADS = 8
B_KV_HEADS = 2
Q_COLS = 1024
K_COLS = 640
V_COLS = 640
ONES_ROWS = 16
MLP_HIDDEN = 4 * D_MODEL
ROPE_THETA = 10000.0
ROPE_FREQS = HEAD_DIM // 4
N_MOD = 6
EPS = 1e-6

TILE = 256
LANES = 128
SUBLANES = 8
HALO = SUBLANES
HID_CHUNK = 1024
KEY_CHUNK = 256
EXP_ROWS = 64
NEG_BIG = -1e30
VMEM_LIMIT = 56 * 1024 * 1024


def _const_spec(shape):
    nd = len(shape)
    return pl.BlockSpec(shape, lambda *_: (0,) * nd, pipeline_mode=pl.Buffered(1))


def _layer_spec(stacked, l):
    nd = stacked.ndim
    return pl.BlockSpec((None,) + stacked.shape[1:], lambda *_: (l,) + (0,) * (nd - 1),
                        pipeline_mode=pl.Buffered(1))


def _stream_specs(x_ctx, x_lat):
    d = x_ctx.shape[-1]
    n_ctx = CTX_LEN // TILE
    return [pl.BlockSpec((None, TILE, d), lambda b, i: (b, 0, 0)),
            pl.BlockSpec((None, TILE, d), lambda b, i: (b, jnp.maximum(i - n_ctx, 0), 0))]


def _read_stream(ctx_ref, lat_ref):
    return jnp.where(pl.program_id(1) == 0, ctx_ref[...], lat_ref[...])


def _params(*sem):
    return pltpu.CompilerParams(dimension_semantics=sem, vmem_limit_bytes=VMEM_LIMIT)


def _ada_kernel(c_ref, w_ref, b_ref, o_ref):
    c = c_ref[...]
    act = c * (1.0 / (1.0 + jnp.exp(-c)))
    o_ref[...] = jnp.dot(act.astype(BF16), w_ref[...].astype(BF16),
                         preferred_element_type=F32) + b_ref[...]


def _ada_modulation(c_all, ada_w, ada_b):
    depth, d, n = ada_w.shape
    rows = c_all.shape[0]
    tn = 1536
    return pl.pallas_call(
        _ada_kernel,
        grid=(depth, n // tn),
        in_specs=[pl.BlockSpec((rows, d), lambda l, j: (0, 0)),
                  pl.BlockSpec((None, d, tn), lambda l, j: (l, 0, j)),
                  pl.BlockSpec((None, 1, tn), lambda l, j: (l, 0, j))],
        out_specs=pl.BlockSpec((None, rows, tn), lambda l, j: (l, 0, j)),
        out_shape=jax.ShapeDtypeStruct((depth, rows, n), F32),
        compiler_params=_params("arbitrary", "arbitrary"),
    )(c_all, ada_w, ada_b.reshape(depth, 1, n))


def _layer_norm(v, g, b):
    mu = jnp.mean(v, axis=-1, keepdims=True)
    d = v - mu
    var = jnp.mean(d * d, axis=-1, keepdims=True)
    return d * lax.rsqrt(var + EPS) * g + b


def _post_tiles(xs, y_fns, mod_ref, ln_ref, w1_ref, w2_ref, alpha, write):
    n_c = MLP_HIDDEN // HID_CHUNK

    def ln1(x, y):
        x1 = _layer_norm(alpha * x + mod_ref[2:3, :] * y, ln_ref[0:1, :], ln_ref[1:2, :])
        return x1, (x1 * (1.0 + mod_ref[4:5, :]) + mod_ref[3:4, :]).astype(BF16)

    def mlp(h2, chunks, y2):
        for c in chunks:
            sl = slice(c * HID_CHUNK, (c + 1) * HID_CHUNK)
            hid = jnp.maximum(jnp.dot(h2, w1_ref[:, sl], preferred_element_type=F32), 0.0)
            part = jnp.dot((hid * hid).astype(BF16), w2_ref[sl, :], preferred_element_type=F32)
            y2 = part if y2 is None else y2 + part
        return y2

    def ln2(x1, y2):
        return _layer_norm(alpha * x1 + mod_ref[5:6, :] * y2, ln_ref[2:3, :], ln_ref[3:4, :])

    x1a, h2a = ln1(xs[0], y_fns[0]())
    if len(xs) == 1:
        write(0, ln2(x1a, mlp(h2a, range(n_c), None)))
        return
    yb = y_fns[1]()
    y2a = mlp(h2a, range(0, n_c // 2), None)
    x1b, h2b = ln1(xs[1], yb)
    y2a = mlp(h2a, range(n_c // 2, n_c), y2a)
    y2b = mlp(h2b, range(0, n_c // 2), None)
    write(0, ln2(x1a, y2a))
    y2b = mlp(h2b, range(n_c // 2, n_c), y2b)
    write(1, ln2(x1b, y2b))


def _tiles_per_step(n_tiles):
    return 2 if n_tiles % 2 == 0 else 1


def _mod_row_map(l, n_batch, kind):
    if kind == "both":
        return lambda b, i: (l, jnp.where(i == 0, n_batch, b), 0, 0)
    if kind == "context":
        return lambda b, i: (l, n_batch, 0, 0)
    return lambda b, i: (l, b, 0, 0)


def _vt_row_layout():
    blocks, base = [], 0
    for n in [A_VDIM] * A_HEADS + [HEAD_DIM] * B_KV_HEADS:
        blocks.append((base, n))
        base += n + ONES_ROWS
    return blocks, base


VT_BLOCKS, VT_ROWS = _vt_row_layout()


def _rope(z, cos, s_up, s_dn):
    n = z.shape[-1]
    return z * cos + pltpu.roll(z, n - ROPE_FREQS, 1) * s_up + pltpu.roll(z, ROPE_FREQS, 1) * s_dn


def _group_mean_sq(z, gsum_ref):
    z2 = z * z
    hi = z2.astype(BF16)
    lo = (z2 - hi.astype(F32)).astype(BF16)
    g = gsum_ref[...]
    ss = jnp.dot(hi, g, preferred_element_type=F32) + jnp.dot(lo, g, preferred_element_type=F32)
    return ss * (1.0 / HEAD_DIM)


def _proj_kernel(xc_ref, xl_ref, mod_ref, w_ref, cos_ref, sup_ref, sdn_ref, qg_ref, kg_ref, gsum_ref,
                 q_ref, k_ref, vt_ref):
    h = (_read_stream(xc_ref, xl_ref) * (1.0 + mod_ref[1:2, :]) + mod_ref[0:1, :]).astype(BF16)
    cos, s_up, s_dn = cos_ref[...], sup_ref[...], sdn_ref[...]
    scale = HEAD_DIM ** -0.5 * math.log2(math.e)

    def lane_groups(z, fn):
        return [fn(z[:, g * LANES:(g + 1) * LANES]) for g in range(z.shape[1] // LANES)]

    def rope_all(z, mul):
        return jnp.concatenate(lane_groups(z, lambda t: _rope(t, cos, s_up, s_dn) * mul), axis=1)

    def norm_all(z, gain_ref):
        gain = gain_ref[...]
        return jnp.concatenate(
            lane_groups(z, lambda t: t * lax.rsqrt(_group_mean_sq(t, gsum_ref) + EPS) * gain), axis=1)

    aq = jnp.dot(h, w_ref[:, 0:512], preferred_element_type=F32)
    q_ref[:, 0:512] = rope_all(aq, scale).astype(BF16)
    bq = jnp.dot(h, w_ref[:, 512:1024], preferred_element_type=F32)
    q_ref[:, 512:1024] = rope_all(norm_all(bq, qg_ref), scale).astype(BF16)
    ak = jnp.dot(h, w_ref[:, 1024:1536], preferred_element_type=F32)
    k_ref[:, 0:512] = rope_all(ak, 1.0).astype(BF16)
    bk = jnp.dot(h, w_ref[:, 1536:1664], preferred_element_type=F32)
    k_ref[:, 512:640] = rope_all(norm_all(bk, kg_ref), 1.0).astype(BF16)
    v = jnp.dot(h, w_ref[:, 1664:2304], preferred_element_type=F32)
    vt = v.T
    ones = jnp.ones((ONES_ROWS, TILE), F32)
    pieces, col = [], 0
    for _, n in VT_BLOCKS:
        pieces += [vt[col:col + n], ones]
        col += n
    vt_ref[...] = jnp.concatenate(pieces, axis=0).astype(BF16)


def _even_projection(xs, mod, l, w_in, e, cos_t, sup_t, sdn_t, qg, kg, gsum, n_batch):
    x_specs = _stream_specs(*xs)
    bsz, d = xs[0].shape[0], xs[0].shape[-1]
    t_all = xs[0].shape[1] + xs[1].shape[1]
    nt = t_all // TILE
    return pl.pallas_call(
        _proj_kernel,
        grid=(bsz, nt),
        in_specs=x_specs + [
                  pl.BlockSpec((None, None, N_MOD, d), _mod_row_map(l, n_batch, "both")),
                  _layer_spec(w_in, e),
                  pl.BlockSpec((TILE, LANES), lambda b, i: (i, 0)),
                  pl.BlockSpec((TILE, LANES), lambda b, i: (i, 0)),
                  pl.BlockSpec((TILE, LANES), lambda b, i: (i, 0)),
                  _const_spec(qg.shape), _const_spec(kg.shape), _const_spec(gsum.shape)],
        out_specs=[pl.BlockSpec((None, TILE, Q_COLS), lambda b, i: (b, i, 0)),
                   pl.BlockSpec((None, TILE, K_COLS), lambda b, i: (b, i, 0)),
                   pl.BlockSpec((None, VT_ROWS, TILE), lambda b, i: (b, 0, i))],
        out_shape=[jax.ShapeDtypeStruct((bsz, t_all, Q_COLS), BF16),
                   jax.ShapeDtypeStruct((bsz, t_all, K_COLS), BF16),
                   jax.ShapeDtypeStruct((bsz, VT_ROWS, t_all), BF16)],
        compiler_params=_params("arbitrary", "arbitrary"),
    )(*xs, mod, w_in, cos_t, sup_t, sdn_t, qg, kg, gsum)


def _key_chunks(n_keys):
    sizes = ([n_keys % KEY_CHUNK] if n_keys % KEY_CHUNK else []) + [KEY_CHUNK] * (n_keys // KEY_CHUNK)
    starts = np.cumsum([0] + sizes[:-1])
    return [(int(a), int(n)) for a, n in zip(starts, sizes)]


def _attn_kernel(q_ref, k_ref, vt_ref, lamv_ref, subg_ref, o_ref, ot_ref, s0_ref, s1_ref, p0_ref, p1_ref,
                 *, lam_init):
    s_refs = (s0_ref, s1_ref)
    p_refs = (p0_ref, p1_ref)
    chunks = _key_chunks(k_ref.shape[0])
    lower = lax.broadcasted_iota(jnp.int32, (TILE, LANES), 1) < HEAD_DIM
    zero = jnp.zeros((TILE, LANES), BF16)

    lv = lamv_ref[...]
    lam = (jnp.exp(jnp.sum(lv[0:1] * lv[1:2], axis=-1, keepdims=True))
           - jnp.exp(jnp.sum(lv[2:3] * lv[3:4], axis=-1, keepdims=True)) + lam_init)
    subg = subg_ref[...]

    q0 = A_HEADS * LANES
    pairs = [(slice(hd * LANES, (hd + 1) * LANES), slice(hd * LANES, (hd + 1) * LANES),
              VT_BLOCKS[hd], VT_BLOCKS[hd]) for hd in range(A_HEADS)]
    pairs += [(slice(q0 + j * LANES, q0 + (j + 1) * LANES), slice(q0, q0 + LANES),
               VT_BLOCKS[A_HEADS], VT_BLOCKS[A_HEADS + 1]) for j in range(B_HEADS // 2)]
    n_pairs = len(pairs)

    def masked_q(i):
        qt = q_ref[:, pairs[i][0]]
        return jnp.concatenate([jnp.where(lower, qt, zero), jnp.where(lower, zero, qt)], axis=0)

    def score_chunk(i, qcat, start, size, m8):
        s = lax.dot_general(k_ref[start:start + size, pairs[i][1]], qcat, (((1,), (1,)), ((), ())),
                            preferred_element_type=F32)
        s_refs[i % 2][start:start + size, :] = s
        return jnp.maximum(m8, jnp.max(s.reshape(size // SUBLANES, SUBLANES, 2 * TILE), axis=0))

    def exp_chunk(i, start, size, m):
        for r in range(start, start + size, EXP_ROWS):
            p_refs[i % 2][r:r + EXP_ROWS, :] = jnp.exp2(s_refs[i % 2][r:r + EXP_ROWS, :] - m).astype(BF16)

    def value_chunk(i, start, size, accs):
        _, _, blk_lo, blk_hi = pairs[i]
        p_ref = p_refs[i % 2]
        if blk_lo == blk_hi:
            base, n = blk_lo
            both = jnp.dot(vt_ref[base:base + n + ONES_ROWS, start:start + size], p_ref[start:start + size, :],
                           preferred_element_type=F32)
            parts = [both[:, :TILE], both[:, TILE:]]
        else:
            parts = [jnp.dot(vt_ref[base:base + n + ONES_ROWS, start:start + size],
                             p_ref[start:start + size, half * TILE:(half + 1) * TILE],
                             preferred_element_type=F32)
                     for half, (base, n) in enumerate((blk_lo, blk_hi))]
        return parts if accs is None else [a + b for a, b in zip(accs, parts)]

    def finish(i, accs):
        o_lo, o_hi = [acc[:n] * (1.0 / acc[n:n + 1]) for (_, n), acc in zip(pairs[i][2:], accs)]
        if i < A_HEADS:
            o = o_lo - lam * o_hi
            ms = jnp.mean(o * o, axis=0, keepdims=True)
            ot_ref[i * A_VDIM:(i + 1) * A_VDIM, :] = o * lax.rsqrt(ms + EPS) * subg * (1.0 - lam_init)
        else:
            base = q0 + (i - A_HEADS) * LANES
            ot_ref[base:base + HEAD_DIM, :] = o_lo
            ot_ref[base + HEAD_DIM:base + LANES, :] = o_hi

    m_prev = None
    for t in range(n_pairs + 2):
        qcat = masked_q(t) if t < n_pairs else None
        m8 = jnp.full((SUBLANES, 2 * TILE), NEG_BIG, F32)
        accs = None
        for start, size in chunks:
            if t < n_pairs:
                m8 = score_chunk(t, qcat, start, size, m8)
            if 1 <= t <= n_pairs:
                exp_chunk(t - 1, start, size, m_prev)
            if 2 <= t:
                accs = value_chunk(t - 2, start, size, accs)
        if 2 <= t:
            finish(t - 2, accs)
        m_prev = jnp.max(m8, axis=0, keepdims=True)
    o_ref[...] = ot_ref[...].T.astype(BF16)


def _attention(q_all, k_all, vt_all, lam_vecs, subg, lam_init, context):
    bsz, t_all, _ = q_all.shape
    n_keys = CTX_LEN if context else t_all
    nq = 1 if context else (t_all - CTX_LEN) // TILE
    q_off = 0 if context else CTX_LEN // TILE
    kern = functools.partial(_attn_kernel, lam_init=lam_init)
    per_batch = dict(pipeline_mode=pl.Buffered(1))
    return pl.pallas_call(
        kern,
        grid=(bsz, nq),
        in_specs=[pl.BlockSpec((None, TILE, Q_COLS), lambda b, i: (b, i + q_off, 0)),
                  pl.BlockSpec((None, n_keys, K_COLS), lambda b, i: (b, 0, 0), **per_batch),
                  pl.BlockSpec((None, VT_ROWS, n_keys), lambda b, i: (b, 0, 0), **per_batch),
                  _const_spec(lam_vecs.shape), _const_spec(subg.shape)],
        out_specs=pl.BlockSpec((None, TILE, Q_COLS), lambda b, i: (b, i, 0)),
        out_shape=jax.ShapeDtypeStruct((bsz, nq * TILE, Q_COLS), BF16),
        scratch_shapes=[pltpu.VMEM((Q_COLS, TILE), F32),
                        pltpu.VMEM((n_keys, 2 * TILE), F32),
                        pltpu.VMEM((n_keys, 2 * TILE), F32),
                        pltpu.VMEM((n_keys, 2 * TILE), BF16),
                        pltpu.VMEM((n_keys, 2 * TILE), BF16)],
        compiler_params=_params("arbitrary", "arbitrary"),
    )(q_all, k_all, vt_all, lam_vecs, subg)


def _post_kernel(o_ref, x_ref, mod_ref, ln_ref, wo_ref, w1_ref, w2_ref, out_ref, *, alpha):
    n = o_ref.shape[0] // TILE
    rows = [slice(k * TILE, (k + 1) * TILE) for k in range(n)]

    def write(k, v):
        out_ref[rows[k], :] = v

    y_fns = [lambda r=r: jnp.dot(o_ref[r, :], wo_ref[...], preferred_element_type=F32) for r in rows]
    _post_tiles([x_ref[r, :] for r in rows], y_fns, mod_ref, ln_ref, w1_ref, w2_ref, alpha, write)


def _even_post(o, x, mod, ln, l, w_out, e, w1, w2, n_batch, kind, alpha):
    bsz, t, d = x.shape
    rows = TILE * _tiles_per_step(t // TILE)
    blk = pl.BlockSpec((None, rows, d), lambda b, i: (b, i, 0))
    return pl.pallas_call(
        functools.partial(_post_kernel, alpha=alpha),
        grid=(bsz, t // rows),
        in_specs=[blk, blk,
                  pl.BlockSpec((None, None, N_MOD, d), _mod_row_map(l, n_batch, kind)),
                  _layer_spec(ln, l), _layer_spec(w_out, e), _layer_spec(w1, l), _layer_spec(w2, l)],
        out_specs=blk,
        out_shape=jax.ShapeDtypeStruct((bsz, t, d), F32),
        compiler_params=_params("arbitrary", "arbitrary"),
    )(o, x, mod, ln, w_out, w1, w2)


def _odd_kernel(x_ref, xp_ref, xn_ref, mod_ref, ln_ref, wi_ref, cw_ref, cb_ref, wo_ref, w1_ref, w2_ref,
                out_ref, u_ref, *, alpha):
    n_rows = x_ref.shape[0]
    rows = [slice(k * TILE, (k + 1) * TILE) for k in range(n_rows // TILE)]
    i = pl.program_id(1)
    first = i == 0
    last = i == pl.num_programs(1) - 1
    x = x_ref[...]
    shift, scale1 = mod_ref[0:1, :], 1.0 + mod_ref[1:2, :]
    x_ext = jnp.concatenate([xp_ref[...], x, xn_ref[...]], axis=0)
    h_ext = (x_ext * scale1 + shift).astype(BF16)
    z_c = jnp.dot(h_ext, wi_ref[:, D_MODEL:2 * D_MODEL], preferred_element_type=F32)
    z_x = jnp.dot(h_ext, wi_ref[:, 2 * D_MODEL:3 * D_MODEL], preferred_element_type=F32)
    row = lax.broadcasted_iota(jnp.int32, (n_rows + 2 * HALO, 1), 0)
    live_from = jnp.where(first, HALO, 0)
    live_to = jnp.where(last, HALO + n_rows, n_rows + 2 * HALO)
    live = jnp.logical_and(row >= live_from, row < live_to)
    u_ref[...] = jnp.where(live, z_c * z_x, 0.0)

    def mixer(r):
        lo = HALO + r.start
        conv = (u_ref[lo - 1:lo - 1 + TILE, :] * cw_ref[0:1, :] + u_ref[lo:lo + TILE, :] * cw_ref[1:2, :]
                + u_ref[lo + 1:lo + 1 + TILE, :] * cw_ref[2:3, :] + cb_ref[...])
        h = (x_ref[r, :] * scale1 + shift).astype(BF16)
        z_b = jnp.dot(h, wi_ref[:, 0:D_MODEL], preferred_element_type=F32)
        return jnp.dot((z_b * conv).astype(BF16), wo_ref[...], preferred_element_type=F32)

    def write(k, v):
        out_ref[rows[k], :] = v

    _post_tiles([x_ref[r, :] for r in rows], [lambda r=r: mixer(r) for r in rows],
                mod_ref, ln_ref, w1_ref, w2_ref, alpha, write)


def _odd_layer(x, mod, ln, l, w_in, conv_w, conv_b, w_out, od, w1, w2, n_batch, kind, alpha):
    bsz, t, d = x.shape
    rows = TILE * _tiles_per_step(t // TILE)
    per_step = rows // HALO
    n_halo_blocks = t // HALO
    blk = pl.BlockSpec((None, rows, d), lambda b, i: (b, i, 0))
    return pl.pallas_call(
        functools.partial(_odd_kernel, alpha=alpha),
        grid=(bsz, t // rows),
        in_specs=[blk,
                  pl.BlockSpec((None, HALO, d), lambda b, i: (b, jnp.maximum(i * per_step - 1, 0), 0)),
                  pl.BlockSpec((None, HALO, d),
                               lambda b, i: (b, jnp.minimum((i + 1) * per_step, n_halo_blocks - 1), 0)),
                  pl.BlockSpec((None, None, N_MOD, d), _mod_row_map(l, n_batch, kind)),
                  _layer_spec(ln, l), _layer_spec(w_in, od), _layer_spec(conv_w, od),
                  _layer_spec(conv_b, od), _layer_spec(w_out, od),
                  _layer_spec(w1, l), _layer_spec(w2, l)],
        out_specs=blk,
        out_shape=jax.ShapeDtypeStruct((bsz, t, d), F32),
        scratch_shapes=[pltpu.VMEM((rows + 2 * HALO, d), F32)],
        compiler_params=_params("arbitrary", "arbitrary"),
    )(x, x, x, mod, ln, w_in, conv_w, conv_b, w_out, w1, w2)


def _rope_tables(n_lat):
    t = jnp.arange(n_lat)
    pos = jnp.stack([(t // GRID_W).astype(F32), (t % GRID_W).astype(F32)], axis=-1)
    inv_freq = ROPE_THETA ** (-jnp.arange(ROPE_FREQS, dtype=F32) / ROPE_FREQS)
    ang = pos[:, :, None] * inv_freq
    cos = jnp.repeat(jnp.cos(ang), 2, axis=1).reshape(n_lat, HEAD_DIM)
    sin = jnp.repeat(jnp.sin(ang), 2, axis=1).reshape(n_lat, HEAD_DIM)
    first_half = (np.arange(HEAD_DIM) % (2 * ROPE_FREQS)) < ROPE_FREQS
    s_up = jnp.where(first_half, -sin, 0.0)
    s_dn = jnp.where(first_half, 0.0, sin)
    reps = LANES // HEAD_DIM
    pad = lambda a, fill: jnp.concatenate(
        [jnp.full((CTX_LEN, LANES), fill, F32), jnp.tile(a, (1, reps))], axis=0)
    return pad(cos, 1.0), pad(s_up, 0.0), pad(s_dn, 0.0)


def _gqa_head_order():
    half = B_HEADS // 2
    return [h for j in range(half) for h in (j, j + half)]


def kernel(x, c, ctx, c_ctx, ada_w, ada_b, attn_w_in, attn_w_out, diff_lambda, diff_subln_g, q_norm_g, k_norm_g, conv_w_in, conv_w, conv_b, conv_w_out, mlp_w1, mlp_w2, ln_g, ln_b):
    bsz, n_lat, d = x.shape
    depth = ada_w.shape[0]
    assert d == D_MODEL and ctx.shape[1] == CTX_LEN == TILE and n_lat % TILE == 0
    alpha = (2 * depth) ** 0.25

    n_rows = -(-(bsz + 1) // SUBLANES) * SUBLANES
    c_all = jnp.concatenate([c, c_ctx[None, :], jnp.zeros((n_rows - bsz - 1, d), F32)], axis=0)
    mod = _ada_modulation(c_all, ada_w, ada_b).reshape(depth, n_rows, N_MOD, d)

    cos_t, sup_t, sdn_t = _rope_tables(n_lat)
    order = _gqa_head_order()
    bq_segs = [(1536 + HEAD_DIM * h, 1536 + HEAD_DIM * (h + 1)) for h in order]
    in_segs = [(0, 512)] + bq_segs + [(512, 1024), (2048, 2176), (1024, 1536), (2176, 2304)]
    out_segs = [(0, 512)] + [(512 + HEAD_DIM * h, 512 + HEAD_DIM * (h + 1)) for h in order]
    gsum = jnp.asarray(np.kron(np.eye(LANES // HEAD_DIM), np.ones((HEAD_DIM, HEAD_DIM))), BF16)
    ln = jnp.stack([ln_g[:, 0], ln_b[:, 0], ln_g[:, 1], ln_b[:, 1]], axis=1)

    w1_all, w2_all = mlp_w1.astype(BF16), mlp_w2.astype(BF16)
    attn_in = jnp.concatenate([attn_w_in[:, :, a:b] for a, b in in_segs], axis=2).astype(BF16)
    attn_out = jnp.concatenate([attn_w_out[:, a:b, :] for a, b in out_segs], axis=1).astype(BF16)
    conv_in, conv_out = conv_w_in.astype(BF16), conv_w_out.astype(BF16)
    conv_bias = conv_b[:, None, :]
    tile2 = lambda g: jnp.tile(g, LANES // HEAD_DIM)[None, :]

    x_ctx, x_lat = ctx, x
    for l in range(depth):
        ctx_out = any(j % 2 == 0 for j in range(l + 1, depth))
        if l % 2 == 0:
            e = l // 2
            lam_init = 0.8 - 0.6 * math.exp(-0.3 * l)
            q_all, k_all, vt_all = _even_projection((x_ctx, x_lat), mod, l, attn_in, e, cos_t, sup_t, sdn_t,
                                                    tile2(q_norm_g[e]), tile2(k_norm_g[e]), gsum, bsz)
            attend = functools.partial(_attention, q_all, k_all, vt_all, diff_lambda[e],
                                       diff_subln_g[e][:, None], lam_init)
            post = functools.partial(_even_post, mod=mod, ln=ln, l=l, w_out=attn_out, e=e, w1=w1_all,
                                     w2=w2_all, n_batch=bsz, alpha=alpha)
            x_lat = post(attend(False), x_lat, kind="latent")
            x_ctx = post(attend(True), x_ctx, kind="context") if ctx_out else None
        else:
            od = l // 2
            odd = functools.partial(_odd_layer, mod=mod, ln=ln, l=l, w_in=conv_in, conv_w=conv_w,
                                    conv_b=conv_bias, w_out=conv_out, od=od, w1=w1_all, w2=w2_all,
                                    n_batch=bsz, alpha=alpha)
            x_lat = odd(x_lat, kind="latent")
            x_ctx = odd(x_ctx, kind="context") if ctx_out else None
    return x_lat
```

```python
import functools
import math

import numpy as np
import jax
import jax.numpy as jnp
from jax import lax
from jax.experimental import pallas as pl
from jax.experimental.pallas import tpu as pltpu

F32 = jnp.float32
BF16 = jnp.bfloat16

D_MODEL = 1024
CTX_LEN = 256
GRID_W = 64
HEAD_DIM = 64
A_HEADS = 4
A_VDIM = 2 * HEAD_DIM
B_HEADS = 8
B_KV_HEADS = 2
Q_COLS = 1024
K_COLS = 640
V_COLS = 640
ONES_ROWS = 16
MLP_HIDDEN = 4 * D_MODEL
ROPE_THETA = 10000.0
ROPE_FREQS = HEAD_DIM // 4
N_MOD = 6
EPS = 1e-6

TILE = 256
LANES = 128
SUBLANES = 8
HALO = SUBLANES
HID_CHUNK = 1024
KEY_CHUNK = 256
N_MAPS = 2 * A_HEADS + B_HEADS
SHIFT_MARGIN = 1.03
DEN_SAFE_MIN = 2.0 ** -60
DEN_SAFE_MAX = 2.0 ** 60
NEG_BIG = -1e30
VMEM_LIMIT = 56 * 1024 * 1024


def _const_spec(shape):
    nd = len(shape)
    return pl.BlockSpec(shape, lambda *_: (0,) * nd, pipeline_mode=pl.Buffered(1))


def _layer_spec(stacked, l):
    nd = stacked.ndim
    return pl.BlockSpec((None,) + stacked.shape[1:], lambda *_: (l,) + (0,) * (nd - 1),
                        pipeline_mode=pl.Buffered(1))


def _stream_specs(x_ctx, x_lat):
    d = x_ctx.shape[-1]
    n_ctx = CTX_LEN // TILE
    return [pl.BlockSpec((None, TILE, d), lambda b, i: (b, 0, 0)),
            pl.BlockSpec((None, TILE, d), lambda b, i: (b, jnp.maximum(i - n_ctx, 0), 0))]


def _read_stream(ctx_ref, lat_ref):
    return jnp.where(pl.program_id(1) == 0, ctx_ref[...], lat_ref[...])


def _params(*sem):
    return pltpu.CompilerParams(dimension_semantics=sem, vmem_limit_bytes=VMEM_LIMIT)


def _ada_kernel(c_ref, w_ref, b_ref, o_ref):
    c = c_ref[...]
    act = c * (1.0 / (1.0 + jnp.exp(-c)))
    o_ref[...] = jnp.dot(act.astype(BF16), w_ref[...].astype(BF16),
                         preferred_element_type=F32) + b_ref[...]


def _ada_modulation(c_all, ada_w, ada_b):
    depth, d, n = ada_w.shape
    rows = c_all.shape[0]
    tn = 1536
    return pl.pallas_call(
        _ada_kernel,
        grid=(depth, n // tn),
        in_specs=[pl.BlockSpec((rows, d), lambda l, j: (0, 0)),
                  pl.BlockSpec((None, d, tn), lambda l, j: (l, 0, j)),
                  pl.BlockSpec((None, 1, tn), lambda l, j: (l, 0, j))],
        out_specs=pl.BlockSpec((None, rows, tn), lambda l, j: (l, 0, j)),
        out_shape=jax.ShapeDtypeStruct((depth, rows, n), F32),
        compiler_params=_params("arbitrary", "arbitrary"),
    )(c_all, ada_w, ada_b.reshape(depth, 1, n))


def _layer_norm(v, g, b):
    mu = jnp.mean(v, axis=-1, keepdims=True)
    d = v - mu
    var = jnp.mean(d * d, axis=-1, keepdims=True)
    return d * lax.rsqrt(var + EPS) * g + b


def _post_tiles(xs, y_fns, mod_ref, ln_ref, w1_ref, w2_ref, alpha, write):
    n_c = MLP_HIDDEN // HID_CHUNK

    def ln1(x, y):
        x1 = _layer_norm(alpha * x + mod_ref[2:3, :] * y, ln_ref[0:1, :], ln_ref[1:2, :])
        return x1, (x1 * (1.0 + mod_ref[4:5, :]) + mod_ref[3:4, :]).astype(BF16)

    def mlp(h2, chunks, y2):
        for c in chunks:
            sl = slice(c * HID_CHUNK, (c + 1) * HID_CHUNK)
            hid = jnp.maximum(jnp.dot(h2, w1_ref[:, sl], preferred_element_type=F32), 0.0)
            part = jnp.dot((hid * hid).astype(BF16), w2_ref[sl, :], preferred_element_type=F32)
            y2 = part if y2 is None else y2 + part
        return y2

    def ln2(x1, y2):
        return _layer_norm(alpha * x1 + mod_ref[5:6, :] * y2, ln_ref[2:3, :], ln_ref[3:4, :])

    x1a, h2a = ln1(xs[0], y_fns[0]())
    if len(xs) == 1:
        write(0, ln2(x1a, mlp(h2a, range(n_c), None)))
        return
    yb = y_fns[1]()
    y2a = mlp(h2a, range(0, n_c // 2), None)
    x1b, h2b = ln1(xs[1], yb)
    y2a = mlp(h2a, range(n_c // 2, n_c), y2a)
    y2b = mlp(h2b, range(0, n_c // 2), None)
    write(0, ln2(x1a, y2a))
    y2b = mlp(h2b, range(n_c // 2, n_c), y2b)
    write(1, ln2(x1b, y2b))


def _tiles_per_step(n_tiles):
    return 2 if n_tiles % 2 == 0 else 1


def _mod_row_map(l, n_batch, kind):
    if kind == "both":
        return lambda b, i: (l, jnp.where(i == 0, n_batch, b), 0, 0)
    if kind == "context":
        return lambda b, i: (l, n_batch, 0, 0)
    return lambda b, i: (l, b, 0, 0)


def _vt_row_layout():
    blocks, base = [], 0
    for n in [A_VDIM] * A_HEADS + [HEAD_DIM] * B_KV_HEADS:
        blocks.append((base, n))
        base += n + ONES_ROWS
    return blocks, base


VT_BLOCKS, VT_ROWS = _vt_row_layout()


def _rope(z, cos, s_up, s_dn):
    n = z.shape[-1]
    return z * cos + pltpu.roll(z, n - ROPE_FREQS, 1) * s_up + pltpu.roll(z, ROPE_FREQS, 1) * s_dn


def _group_mean_sq(z, gsum_ref):
    z2 = z * z
    hi = z2.astype(BF16)
    lo = (z2 - hi.astype(F32)).astype(BF16)
    g = gsum_ref[...]
    ss = jnp.dot(hi, g, preferred_element_type=F32) + jnp.dot(lo, g, preferred_element_type=F32)
    return ss * (1.0 / HEAD_DIM)


def _proj_kernel(xc_ref, xl_ref, mod_ref, w_ref, cos_ref, sup_ref, sdn_ref, qg_ref, kg_ref, gsum_ref, msel_ref,
                 q_ref, k_ref, vt_ref, qn_ref, kn_ref):
    h = (_read_stream(xc_ref, xl_ref) * (1.0 + mod_ref[1:2, :]) + mod_ref[0:1, :]).astype(BF16)
    cos, s_up, s_dn = cos_ref[...], sup_ref[...], sdn_ref[...]
    scale = HEAD_DIM ** -0.5 * math.log2(math.e)

    def lane_groups(z, fn):
        return [fn(z[:, g * LANES:(g + 1) * LANES]) for g in range(z.shape[1] // LANES)]

    def rope_all(z, mul):
        return jnp.concatenate(lane_groups(z, lambda t: _rope(t, cos, s_up, s_dn) * mul), axis=1)

    def norm_all(z, gain_ref):
        gain = gain_ref[...]
        return jnp.concatenate(
            lane_groups(z, lambda t: t * lax.rsqrt(_group_mean_sq(t, gsum_ref) + EPS) * gain), axis=1)

    aq = rope_all(jnp.dot(h, w_ref[:, 0:512], preferred_element_type=F32), scale)
    q_ref[:, 0:512] = aq.astype(BF16)
    bq = rope_all(norm_all(jnp.dot(h, w_ref[:, 512:1024], preferred_element_type=F32), qg_ref), scale)
    q_ref[:, 512:1024] = bq.astype(BF16)
    ak = rope_all(jnp.dot(h, w_ref[:, 1024:1536], preferred_element_type=F32), 1.0)
    k_ref[:, 0:512] = ak.astype(BF16)
    bk = rope_all(norm_all(jnp.dot(h, w_ref[:, 1536:1664], preferred_element_type=F32), kg_ref), 1.0)
    k_ref[:, 512:640] = bk.astype(BF16)

    q_sq = jnp.concatenate([aq * aq, bq * bq], axis=1).astype(BF16)
    qn_ref[...] = lax.dot_general(msel_ref[...], q_sq, (((1,), (1,)), ((), ())), preferred_element_type=F32)
    k_sq = jnp.concatenate([ak * ak, bk * bk], axis=1).astype(BF16)
    g = gsum_ref[...]
    k_n2 = jnp.concatenate(lane_groups(k_sq, lambda t: jnp.dot(t, g, preferred_element_type=F32)), axis=1)
    k_max = jnp.broadcast_to(jnp.max(k_n2, axis=0, keepdims=True), kn_ref.shape)

    @pl.when(pl.program_id(1) == 0)
    def _():
        kn_ref[...] = k_max

    @pl.when(pl.program_id(1) != 0)
    def _():
        kn_ref[...] = jnp.maximum(kn_ref[...], k_max)

    v = jnp.dot(h, w_ref[:, 1664:2304], preferred_element_type=F32)
    vt = v.T
    ones = jnp.ones((ONES_ROWS, TILE), F32)
    pieces, col = [], 0
    for _, n in VT_BLOCKS:
        pieces += [vt[col:col + n], ones]
        col += n
    vt_ref[...] = jnp.concatenate(pieces, axis=0).astype(BF16)


def _even_projection(xs, mod, l, w_in, e, cos_t, sup_t, sdn_t, qg, kg, gsum, msel, n_batch):
    x_specs = _stream_specs(*xs)
    bsz, d = xs[0].shape[0], xs[0].shape[-1]
    t_all = xs[0].shape[1] + xs[1].shape[1]
    nt = t_all // TILE
    return pl.pallas_call(
        _proj_kernel,
        grid=(bsz, nt),
        in_specs=x_specs + [
                  pl.BlockSpec((None, None, N_MOD, d), _mod_row_map(l, n_batch, "both")),
                  _layer_spec(w_in, e),
                  pl.BlockSpec((TILE, LANES), lambda b, i: (i, 0)),
                  pl.BlockSpec((TILE, LANES), lambda b, i: (i, 0)),
                  pl.BlockSpec((TILE, LANES), lambda b, i: (i, 0)),
                  _const_spec(qg.shape), _const_spec(kg.shape), _const_spec(gsum.shape),
                  _const_spec(msel.shape)],
        out_specs=[pl.BlockSpec((None, TILE, Q_COLS), lambda b, i: (b, i, 0)),
                   pl.BlockSpec((None, TILE, K_COLS), lambda b, i: (b, i, 0)),
                   pl.BlockSpec((None, VT_ROWS, TILE), lambda b, i: (b, 0, i)),
                   pl.BlockSpec((None, N_MAPS, TILE), lambda b, i: (b, 0, i)),
                   pl.BlockSpec((None, SUBLANES, K_COLS), lambda b, i: (b, 0, 0))],
        out_shape=[jax.ShapeDtypeStruct((bsz, t_all, Q_COLS), BF16),
                   jax.ShapeDtypeStruct((bsz, t_all, K_COLS), BF16),
                   jax.ShapeDtypeStruct((bsz, VT_ROWS, t_all), BF16),
                   jax.ShapeDtypeStruct((bsz, N_MAPS, t_all), F32),
                   jax.ShapeDtypeStruct((bsz, SUBLANES, K_COLS), F32)],
        compiler_params=_params("arbitrary", "arbitrary"),
    )(*xs, mod, w_in, cos_t, sup_t, sdn_t, qg, kg, gsum, msel)


def _key_chunks(n_keys):
    sizes = ([n_keys % KEY_CHUNK] if n_keys % KEY_CHUNK else []) + [KEY_CHUNK] * (n_keys // KEY_CHUNK)
    starts = np.cumsum([0] + sizes[:-1])
    return [(int(a), int(n)) for a, n in zip(starts, sizes)]


def _attn_kernel(q_ref, k_ref, vt_ref, qn_ref, kn_ref, lamv_ref, subg_ref, o_ref, ot_ref, p0_ref, p1_ref,
                 mx_ref, *, lam_init):
    p_refs = (p0_ref, p1_ref)
    chunks = _key_chunks(k_ref.shape[0])
    lower = lax.broadcasted_iota(jnp.int32, (TILE, LANES), 1) < HEAD_DIM
    zero = jnp.zeros((TILE, LANES), BF16)

    lv = lamv_ref[...]
    lam = (jnp.exp(jnp.sum(lv[0:1] * lv[1:2], axis=-1, keepdims=True))
           - jnp.exp(jnp.sum(lv[2:3] * lv[3:4], axis=-1, keepdims=True)) + lam_init)
    subg = subg_ref[...]

    q0 = A_HEADS * LANES
    pairs = [(slice(hd * LANES, (hd + 1) * LANES), slice(hd * LANES, (hd + 1) * LANES),
              VT_BLOCKS[hd], VT_BLOCKS[hd]) for hd in range(A_HEADS)]
    pairs += [(slice(q0 + j * LANES, q0 + (j + 1) * LANES), slice(q0, q0 + LANES),
               VT_BLOCKS[A_HEADS], VT_BLOCKS[A_HEADS + 1]) for j in range(B_HEADS // 2)]
    n_pairs = len(pairs)

    def masked_q(i):
        qt = q_ref[:, pairs[i][0]]
        return jnp.concatenate([jnp.where(lower, qt, zero), jnp.where(lower, zero, qt)], axis=0)

    def estimate(i):
        k_lo = pairs[i][1].start
        halves = [jnp.sqrt(qn_ref[2 * i + half:2 * i + half + 1, :]
                           * kn_ref[0:1, k_lo + half * HEAD_DIM:k_lo + half * HEAD_DIM + 1])
                  for half in range(2)]
        return jnp.concatenate(halves, axis=1) * SHIFT_MARGIN

    def score_chunk(i, qcat, start, size, shift, m8):
        s = lax.dot_general(k_ref[start:start + size, pairs[i][1]], qcat, (((1,), (1,)), ((), ())),
                            preferred_element_type=F32)
        p_refs[i % 2][start:start + size, :] = jnp.exp2(s - shift).astype(BF16)
        return jnp.maximum(m8, jnp.max(s.reshape(size // SUBLANES, SUBLANES, 2 * TILE), axis=0))

    def value_chunk(i, start, size, accs):
        _, _, blk_lo, blk_hi = pairs[i]
        p_ref = p_refs[i % 2]
        if blk_lo == blk_hi:
            base, n = blk_lo
            both = jnp.dot(vt_ref[base:base + n + ONES_ROWS, start:start + size], p_ref[start:start + size, :],
                           preferred_element_type=F32)
            parts = [both[:, :TILE], both[:, TILE:]]
        else:
            parts = [jnp.dot(vt_ref[base:base + n + ONES_ROWS, start:start + size],
                             p_ref[start:start + size, half * TILE:(half + 1) * TILE],
                             preferred_element_type=F32)
                     for half, (base, n) in enumerate((blk_lo, blk_hi))]
        return parts if accs is None else [a + b for a, b in zip(accs, parts)]

    def finish(i, accs):
        dens = [acc[n:n + 1] for (_, n), acc in zip(pairs[i][2:], accs)]
        o_lo, o_hi = [acc[:n] * (1.0 / den) for (_, n), acc, den in zip(pairs[i][2:], accs, dens)]
        if i < A_HEADS:
            o = o_lo - lam * o_hi
            ms = jnp.mean(o * o, axis=0, keepdims=True)
            ot_ref[i * A_VDIM:(i + 1) * A_VDIM, :] = o * lax.rsqrt(ms + EPS) * subg * (1.0 - lam_init)
        else:
            base = q0 + (i - A_HEADS) * LANES
            ot_ref[base:base + HEAD_DIM, :] = o_lo
            ot_ref[base + HEAD_DIM:base + LANES, :] = o_hi
        return dens

    mx_ref[...] = jnp.zeros_like(mx_ref)

    def attempt(carry):
        n_done, _ = carry
        den_min, den_max = None, None
        for t in range(n_pairs + 1):
            if t < n_pairs:
                qcat = masked_q(t)
                exact = jnp.max(mx_ref[t], axis=0, keepdims=True)
                shift = jnp.where(n_done == 0, estimate(t), exact)
                m8 = jnp.full((SUBLANES, 2 * TILE), NEG_BIG, F32)
            accs = None
            for start, size in chunks:
                if t < n_pairs:
                    m8 = score_chunk(t, qcat, start, size, shift, m8)
                if 1 <= t:
                    accs = value_chunk(t - 1, start, size, accs)
            if t < n_pairs:
                mx_ref[t] = m8
            if 1 <= t:
                for den in finish(t - 1, accs):
                    den_min = den if den_min is None else jnp.minimum(den_min, den)
                    den_max = den if den_max is None else jnp.maximum(den_max, den)
        safe = jnp.logical_and(jnp.min(den_min) >= DEN_SAFE_MIN, jnp.max(den_max) <= DEN_SAFE_MAX)
        return n_done + 1, safe

    def again(carry):
        n_done, safe = carry
        return jnp.logical_or(n_done == 0, jnp.logical_and(n_done == 1, jnp.logical_not(safe)))

    lax.while_loop(again, attempt, (jnp.int32(0), jnp.bool_(False)))
    o_ref[...] = ot_ref[...].T.astype(BF16)


def _attention(q_all, k_all, vt_all, qn_all, kn_all, lam_vecs, subg, lam_init, context):
    bsz, t_all, _ = q_all.shape
    n_keys = CTX_LEN if context else t_all
    nq = 1 if context else (t_all - CTX_LEN) // TILE
    q_off = 0 if context else CTX_LEN // TILE
    kern = functools.partial(_attn_kernel, lam_init=lam_init)
    per_batch = dict(pipeline_mode=pl.Buffered(1))
    return pl.pallas_call(
        kern,
        grid=(bsz, nq),
        in_specs=[pl.BlockSpec((None, TILE, Q_COLS), lambda b, i: (b, i + q_off, 0)),
                  pl.BlockSpec((None, n_keys, K_COLS), lambda b, i: (b, 0, 0), **per_batch),
                  pl.BlockSpec((None, VT_ROWS, n_keys), lambda b, i: (b, 0, 0), **per_batch),
                  pl.BlockSpec((None, N_MAPS, TILE), lambda b, i: (b, 0, i + q_off)),
                  pl.BlockSpec((None, SUBLANES, K_COLS), lambda b, i: (b, 0, 0)),
                  _const_spec(lam_vecs.shape), _const_spec(subg.shape)],
        out_specs=pl.BlockSpec((None, TILE, Q_COLS), lambda b, i: (b, i, 0)),
        out_shape=jax.ShapeDtypeStruct((bsz, nq * TILE, Q_COLS), BF16),
        scratch_shapes=[pltpu.VMEM((Q_COLS, TILE), F32),
                        pltpu.VMEM((n_keys, 2 * TILE), BF16),
                        pltpu.VMEM((n_keys, 2 * TILE), BF16),
                        pltpu.VMEM((N_MAPS // 2, SUBLANES, 2 * TILE), F32)],
        compiler_params=_params("arbitrary", "arbitrary"),
    )(q_all, k_all, vt_all, qn_all, kn_all, lam_vecs, subg)


def _post_kernel(o_ref, x_ref, mod_ref, ln_ref, wo_ref, w1_ref, w2_ref, out_ref, *, alpha):
    n = o_ref.shape[0] // TILE
    rows = [slice(k * TILE, (k + 1) * TILE) for k in range(n)]

    def write(k, v):
        out_ref[rows[k], :] = v

    y_fns = [lambda r=r: jnp.dot(o_ref[r, :], wo_ref[...], preferred_element_type=F32) for r in rows]
    _post_tiles([x_ref[r, :] for r in rows], y_fns, mod_ref, ln_ref, w1_ref, w2_ref, alpha, write)


def _even_post(o, x, mod, ln, l, w_out, e, w1, w2, n_batch, kind, alpha):
    bsz, t, d = x.shape
    rows = TILE * _tiles_per_step(t // TILE)
    blk = pl.BlockSpec((None, rows, d), lambda b, i: (b, i, 0))
    return pl.pallas_call(
        functools.partial(_post_kernel, alpha=alpha),
        grid=(bsz, t // rows),
        in_specs=[blk, blk,
                  pl.BlockSpec((None, None, N_MOD, d), _mod_row_map(l, n_batch, kind)),
                  _layer_spec(ln, l), _layer_spec(w_out, e), _layer_spec(w1, l), _layer_spec(w2, l)],
        out_specs=blk,
        out_shape=jax.ShapeDtypeStruct((bsz, t, d), F32),
        compiler_params=_params("arbitrary", "arbitrary"),
    )(o, x, mod, ln, w_out, w1, w2)


def _odd_kernel(x_ref, xp_ref, xn_ref, mod_ref, ln_ref, wi_ref, cw_ref, cb_ref, wo_ref, w1_ref, w2_ref,
                out_ref, u_ref, *, alpha):
    n_rows = x_ref.shape[0]
    rows = [slice(k * TILE, (k + 1) * TILE) for k in range(n_rows // TILE)]
    i = pl.program_id(1)
    first = i == 0
    last = i == pl.num_programs(1) - 1
    x = x_ref[...]
    shift, scale1 = mod_ref[0:1, :], 1.0 + mod_ref[1:2, :]
    x_ext = jnp.concatenate([xp_ref[...], x, xn_ref[...]], axis=0)
    h_ext = (x_ext * scale1 + shift).astype(BF16)
    z_c = jnp.dot(h_ext, wi_ref[:, D_MODEL:2 * D_MODEL], preferred_element_type=F32)
    z_x = jnp.dot(h_ext, wi_ref[:, 2 * D_MODEL:3 * D_MODEL], preferred_element_type=F32)
    row = lax.broadcasted_iota(jnp.int32, (n_rows + 2 * HALO, 1), 0)
    live_from = jnp.where(first, HALO, 0)
    live_to = jnp.where(last, HALO + n_rows, n_rows + 2 * HALO)
    live = jnp.logical_and(row >= live_from, row < live_to)
    u_ref[...] = jnp.where(live, z_c * z_x, 0.0)

    def mixer(r):
        lo = HALO + r.start
        conv = (u_ref[lo - 1:lo - 1 + TILE, :] * cw_ref[0:1, :] + u_ref[lo:lo + TILE, :] * cw_ref[1:2, :]
                + u_ref[lo + 1:lo + 1 + TILE, :] * cw_ref[2:3, :] + cb_ref[...])
        h = (x_ref[r, :] * scale1 + shift).astype(BF16)
        z_b = jnp.dot(h, wi_ref[:, 0:D_MODEL], preferred_element_type=F32)
        return jnp.dot((z_b * conv).astype(BF16), wo_ref[...], preferred_element_type=F32)

    def write(k, v):
        out_ref[rows[k], :] = v

    _post_tiles([x_ref[r, :] for r in rows], [lambda r=r: mixer(r) for r in rows],
                mod_ref, ln_ref, w1_ref, w2_ref, alpha, write)


def _odd_layer(x, mod, ln, l, w_in, conv_w, conv_b, w_out, od, w1, w2, n_batch, kind, alpha):
    bsz, t, d = x.shape
    rows = TILE * _tiles_per_step(t // TILE)
    per_step = rows // HALO
    n_halo_blocks = t // HALO
    blk = pl.BlockSpec((None, rows, d), lambda b, i: (b, i, 0))
    return pl.pallas_call(
        functools.partial(_odd_kernel, alpha=alpha),
        grid=(bsz, t // rows),
        in_specs=[blk,
                  pl.BlockSpec((None, HALO, d), lambda b, i: (b, jnp.maximum(i * per_step - 1, 0), 0)),
                  pl.BlockSpec((None, HALO, d),
                               lambda b, i: (b, jnp.minimum((i + 1) * per_step, n_halo_blocks - 1), 0)),
                  pl.BlockSpec((None, None, N_MOD, d), _mod_row_map(l, n_batch, kind)),
                  _layer_spec(ln, l), _layer_spec(w_in, od), _layer_spec(conv_w, od),
                  _layer_spec(conv_b, od), _layer_spec(w_out, od),
                  _layer_spec(w1, l), _layer_spec(w2, l)],
        out_specs=blk,
        out_shape=jax.ShapeDtypeStruct((bsz, t, d), F32),
        scratch_shapes=[pltpu.VMEM((rows + 2 * HALO, d), F32)],
        compiler_params=_params("arbitrary", "arbitrary"),
    )(x, x, x, mod, ln, w_in, conv_w, conv_b, w_out, w1, w2)


def _rope_tables(n_lat):
    t = jnp.arange(n_lat)
    pos = jnp.stack([(t // GRID_W).astype(F32), (t % GRID_W).astype(F32)], axis=-1)
    inv_freq = ROPE_THETA ** (-jnp.arange(ROPE_FREQS, dtype=F32) / ROPE_FREQS)
    ang = pos[:, :, None] * inv_freq
    cos = jnp.repeat(jnp.cos(ang), 2, axis=1).reshape(n_lat, HEAD_DIM)
    sin = jnp.repeat(jnp.sin(ang), 2, axis=1).reshape(n_lat, HEAD_DIM)
    first_half = (np.arange(HEAD_DIM) % (2 * ROPE_FREQS)) < ROPE_FREQS
    s_up = jnp.where(first_half, -sin, 0.0)
    s_dn = jnp.where(first_half, 0.0, sin)
    reps = LANES // HEAD_DIM
    pad = lambda a, fill: jnp.concatenate(
        [jnp.full((CTX_LEN, LANES), fill, F32), jnp.tile(a, (1, reps))], axis=0)
    return pad(cos, 1.0), pad(s_up, 0.0), pad(s_dn, 0.0)


def _gqa_head_order():
    half = B_HEADS // 2
    return [h for j in range(half) for h in (j, j + half)]


def kernel(x, c, ctx, c_ctx, ada_w, ada_b, attn_w_in, attn_w_out, diff_lambda, diff_subln_g, q_norm_g, k_norm_g, conv_w_in, conv_w, conv_b, conv_w_out, mlp_w1, mlp_w2, ln_g, ln_b):
    bsz, n_lat, d = x.shape
    depth = ada_w.shape[0]
    assert d == D_MODEL and ctx.shape[1] == CTX_LEN == TILE and n_lat % TILE == 0
    alpha = (2 * depth) ** 0.25

    n_rows = -(-(bsz + 1) // SUBLANES) * SUBLANES
    c_all = jnp.concatenate([c, c_ctx[None, :], jnp.zeros((n_rows - bsz - 1, d), F32)], axis=0)
    mod = _ada_modulation(c_all, ada_w, ada_b).reshape(depth, n_rows, N_MOD, d)

    cos_t, sup_t, sdn_t = _rope_tables(n_lat)
    order = _gqa_head_order()
    bq_segs = [(1536 + HEAD_DIM * h, 1536 + HEAD_DIM * (h + 1)) for h in order]
    in_segs = [(0, 512)] + bq_segs + [(512, 1024), (2048, 2176), (1024, 1536), (2176, 2304)]
    out_segs = [(0, 512)] + [(512 + HEAD_DIM * h, 512 + HEAD_DIM * (h + 1)) for h in order]
    gsum = jnp.asarray(np.kron(np.eye(LANES // HEAD_DIM), np.ones((HEAD_DIM, HEAD_DIM))), BF16)
    msel = jnp.asarray(np.kron(np.eye(N_MAPS), np.ones((1, HEAD_DIM))), BF16)
    ln = jnp.stack([ln_g[:, 0], ln_b[:, 0], ln_g[:, 1], ln_b[:, 1]], axis=1)

    w1_all, w2_all = mlp_w1.astype(BF16), mlp_w2.astype(BF16)
    attn_in = jnp.concatenate([attn_w_in[:, :, a:b] for a, b in in_segs], axis=2).astype(BF16)
    attn_out = jnp.concatenate([attn_w_out[:, a:b, :] for a, b in out_segs], axis=1).astype(BF16)
    conv_in, conv_out = conv_w_in.astype(BF16), conv_w_out.astype(BF16)
    conv_bias = conv_b[:, None, :]
    tile2 = lambda g: jnp.tile(g, LANES // HEAD_DIM)[None, :]

    x_ctx, x_lat = ctx, x
    for l in range(depth):
        ctx_out = any(j % 2 == 0 for j in range(l + 1, depth))
        if l % 2 == 0:
            e = l // 2
            lam_init = 0.8 - 0.6 * math.exp(-0.3 * l)
            q_all, k_all, vt_all, qn_all, kn_all = _even_projection(
                (x_ctx, x_lat), mod, l, attn_in, e, cos_t, sup_t, sdn_t,
                tile2(q_norm_g[e]), tile2(k_norm_g[e]), gsum, msel, bsz)
            attend = functools.partial(_attention, q_all, k_all, vt_all, qn_all, kn_all, diff_lambda[e],
                                       diff_subln_g[e][:, None], lam_init)
            post = functools.partial(_even_post, mod=mod, ln=ln, l=l, w_out=attn_out, e=e, w1=w1_all,
                                     w2=w2_all, n_batch=bsz, alpha=alpha)
            x_lat = post(attend(False), x_lat, kind="latent")
            x_ctx = post(attend(True), x_ctx, kind="context") if ctx_out else None
        else:
            od = l // 2
            odd = functools.partial(_odd_layer, mod=mod, ln=ln, l=l, w_in=conv_in, conv_w=conv_w,
                                    conv_b=conv_bias, w_out=conv_out, od=od, w1=w1_all, w2=w2_all,
                                    n_batch=bsz, alpha=alpha)
            x_lat = odd(x_lat, kind="latent")
            x_ctx = odd(x_ctx, kind="context") if ctx_out else None
    return x_lat
```

```python
import functools
import math

import numpy as np
import jax
import jax.numpy as jnp
from jax import lax
from jax.experimental import pallas as pl
from jax.experimental.pallas import tpu as pltpu

F32 = jnp.float32
BF16 = jnp.bfloat16

D_MODEL = 1024
CTX_LEN = 256
GRID_W = 64
HEAD_DIM = 64
A_HEADS = 4
A_VDIM = 2 * HEAD_DIM
B_HEADS = 8
B_KV_HEADS = 2
Q_COLS = 1024
K_COLS = 640
V_COLS = 640
ONES_ROWS = 16
MLP_HIDDEN = 4 * D_MODEL
ROPE_THETA = 10000.0
ROPE_FREQS = HEAD_DIM // 4
N_MOD = 6
EPS = 1e-6

TILE = 256
LANES = 128
SUBLANES = 8
HALO = SUBLANES
HID_CHUNK = 1024
KEY_CHUNK = 256
N_MAPS = 2 * A_HEADS + B_HEADS
SHIFT_MARGIN = 1.03
DEN_SAFE_MIN = 2.0 ** -60
DEN_SAFE_MAX = 2.0 ** 60
NEG_BIG = -1e30
VMEM_LIMIT = 56 * 1024 * 1024


def _const_spec(shape):
    nd = len(shape)
    return pl.BlockSpec(shape, lambda *_: (0,) * nd, pipeline_mode=pl.Buffered(1))


def _layer_spec(stacked, l):
    nd = stacked.ndim
    return pl.BlockSpec((None,) + stacked.shape[1:], lambda *_: (l,) + (0,) * (nd - 1),
                        pipeline_mode=pl.Buffered(1))


def _stream_specs(x_ctx, x_lat):
    d = x_ctx.shape[-1]
    n_ctx = CTX_LEN // TILE
    return [pl.BlockSpec((None, TILE, d), lambda b, i: (b, 0, 0)),
            pl.BlockSpec((None, TILE, d), lambda b, i: (b, jnp.maximum(i - n_ctx, 0), 0))]


def _read_stream(ctx_ref, lat_ref):
    return jnp.where(pl.program_id(1) == 0, ctx_ref[...], lat_ref[...])


def _params(*sem):
    return pltpu.CompilerParams(dimension_semantics=sem, vmem_limit_bytes=VMEM_LIMIT)


def _ada_kernel(c_ref, w_ref, b_ref, o_ref):
    c = c_ref[...]
    act = c * (1.0 / (1.0 + jnp.exp(-c)))
    o_ref[...] = jnp.dot(act.astype(BF16), w_ref[...].astype(BF16),
                         preferred_element_type=F32) + b_ref[...]


def _ada_modulation(c_all, ada_w, ada_b):
    depth, d, n = ada_w.shape
    rows = c_all.shape[0]
    tn = 1536
    return pl.pallas_call(
        _ada_kernel,
        grid=(depth, n // tn),
        in_specs=[pl.BlockSpec((rows, d), lambda l, j: (0, 0)),
                  pl.BlockSpec((None, d, tn), lambda l, j: (l, 0, j)),
                  pl.BlockSpec((None, 1, tn), lambda l, j: (l, 0, j))],
        out_specs=pl.BlockSpec((None, rows, tn), lambda l, j: (l, 0, j)),
        out_shape=jax.ShapeDtypeStruct((depth, rows, n), F32),
        compiler_params=_params("arbitrary", "arbitrary"),
    )(c_all, ada_w, ada_b.reshape(depth, 1, n))


def _layer_norm(v, g, b):
    mu = jnp.mean(v, axis=-1, keepdims=True)
    d = v - mu
    var = jnp.mean(d * d, axis=-1, keepdims=True)
    return d * lax.rsqrt(var + EPS) * g + b


def _post_tiles(xs, y_fns, mod_ref, ln_ref, w1_ref, w2_ref, alpha, write):
    n_c = MLP_HIDDEN // HID_CHUNK

    def ln1(x, y):
        x1 = _layer_norm(alpha * x + mod_ref[2:3, :] * y, ln_ref[0:1, :], ln_ref[1:2, :])
        return x1, (x1 * (1.0 + mod_ref[4:5, :]) + mod_ref[3:4, :]).astype(BF16)

    def mlp(h2, chunks, y2):
        for c in chunks:
            sl = slice(c * HID_CHUNK, (c + 1) * HID_CHUNK)
            hid = jnp.maximum(jnp.dot(h2, w1_ref[:, sl], preferred_element_type=F32), 0.0)
            part = jnp.dot((hid * hid).astype(BF16), w2_ref[sl, :], preferred_element_type=F32)
            y2 = part if y2 is None else y2 + part
        return y2

    def ln2(x1, y2):
        return _layer_norm(alpha * x1 + mod_ref[5:6, :] * y2, ln_ref[2:3, :], ln_ref[3:4, :])

    x1a, h2a = ln1(xs[0], y_fns[0]())
    if len(xs) == 1:
        write(0, ln2(x1a, mlp(h2a, range(n_c), None)))
        return
    yb = y_fns[1]()
    y2a = mlp(h2a, range(0, n_c // 2), None)
    x1b, h2b = ln1(xs[1], yb)
    y2a = mlp(h2a, range(n_c // 2, n_c), y2a)
    y2b = mlp(h2b, range(0, n_c // 2), None)
    write(0, ln2(x1a, y2a))
    y2b = mlp(h2b, range(n_c // 2, n_c), y2b)
    write(1, ln2(x1b, y2b))


def _tiles_per_step(n_tiles):
    return 2 if n_tiles % 2 == 0 else 1


def _mod_row_map(l, n_batch, kind):
    if kind == "both":
        return lambda b, i: (l, jnp.where(i == 0, n_batch, b), 0, 0)
    if kind == "context":
        return lambda b, i: (l, n_batch, 0, 0)
    return lambda b, i: (l, b, 0, 0)


def _vt_row_layout():
    blocks, base = [], 0
    for n in [A_VDIM] * A_HEADS + [HEAD_DIM] * B_KV_HEADS:
        blocks.append((base, n))
        base += n + ONES_ROWS
    return blocks, base


VT_BLOCKS, VT_ROWS = _vt_row_layout()


def _rope(z, cos, s_up, s_dn):
    n = z.shape[-1]
    return z * cos + pltpu.roll(z, n - ROPE_FREQS, 1) * s_up + pltpu.roll(z, ROPE_FREQS, 1) * s_dn


def _group_mean_sq(z, gsum_ref):
    z2 = z * z
    hi = z2.astype(BF16)
    lo = (z2 - hi.astype(F32)).astype(BF16)
    g = gsum_ref[...]
    ss = jnp.dot(hi, g, preferred_element_type=F32) + jnp.dot(lo, g, preferred_element_type=F32)
    return ss * (1.0 / HEAD_DIM)


def _proj_kernel(xc_ref, xl_ref, mod_ref, w_ref, cos_ref, sup_ref, sdn_ref, qg_ref, kg_ref, gsum_ref, msel_ref,
                 q_ref, k_ref, vt_ref, qn_ref, kn_ref):
    h = (_read_stream(xc_ref, xl_ref) * (1.0 + mod_ref[1:2, :]) + mod_ref[0:1, :]).astype(BF16)
    cos, s_up, s_dn = cos_ref[...], sup_ref[...], sdn_ref[...]
    scale = HEAD_DIM ** -0.5 * math.log2(math.e)

    def lane_groups(z, fn):
        return [fn(z[:, g * LANES:(g + 1) * LANES]) for g in range(z.shape[1] // LANES)]

    def rope_all(z, mul):
        return jnp.concatenate(lane_groups(z, lambda t: _rope(t, cos, s_up, s_dn) * mul), axis=1)

    def norm_all(z, gain_ref):
        gain = gain_ref[...]
        return jnp.concatenate(
            lane_groups(z, lambda t: t * lax.rsqrt(_group_mean_sq(t, gsum_ref) + EPS) * gain), axis=1)

    aq = rope_all(jnp.dot(h, w_ref[:, 0:512], preferred_element_type=F32), scale)
    q_ref[:, 0:512] = aq.astype(BF16)
    bq = rope_all(norm_all(jnp.dot(h, w_ref[:, 512:1024], preferred_element_type=F32), qg_ref), scale)
    q_ref[:, 512:1024] = bq.astype(BF16)
    ak = rope_all(jnp.dot(h, w_ref[:, 1024:1536], preferred_element_type=F32), 1.0)
    k_ref[:, 0:512] = ak.astype(BF16)
    bk = rope_all(norm_all(jnp.dot(h, w_ref[:, 1536:1664], preferred_element_type=F32), kg_ref), 1.0)
    k_ref[:, 512:640] = bk.astype(BF16)

    q_sq = jnp.concatenate([aq * aq, bq * bq], axis=1).astype(BF16)
    qn_ref[...] = lax.dot_general(msel_ref[...], q_sq, (((1,), (1,)), ((), ())), preferred_element_type=F32)
    k_sq = jnp.concatenate([ak * ak, bk * bk], axis=1).astype(BF16)
    g = gsum_ref[...]
    k_n2 = jnp.concatenate(lane_groups(k_sq, lambda t: jnp.dot(t, g, preferred_element_type=F32)), axis=1)
    k_max = jnp.broadcast_to(jnp.max(k_n2, axis=0, keepdims=True), kn_ref.shape)

    @pl.when(pl.program_id(1) == 0)
    def _():
        kn_ref[...] = k_max

    @pl.when(pl.program_id(1) != 0)
    def _():
        kn_ref[...] = jnp.maximum(kn_ref[...], k_max)

    v = jnp.dot(h, w_ref[:, 1664:2304], preferred_element_type=F32)
    vt = v.T
    ones = jnp.ones((ONES_ROWS, TILE), F32)
    pieces, col = [], 0
    for _, n in VT_BLOCKS:
        pieces += [vt[col:col + n], ones]
        col += n
    vt_ref[...] = jnp.concatenate(pieces, axis=0).astype(BF16)


def _even_projection(xs, mod, l, w_in, e, cos_t, sup_t, sdn_t, qg, kg, gsum, msel, n_batch):
    x_specs = _stream_specs(*xs)
    bsz, d = xs[0].shape[0], xs[0].shape[-1]
    t_all = xs[0].shape[1] + xs[1].shape[1]
    nt = t_all // TILE
    return pl.pallas_call(
        _proj_kernel,
        grid=(bsz, nt),
        in_specs=x_specs + [
                  pl.BlockSpec((None, None, N_MOD, d), _mod_row_map(l, n_batch, "both")),
                  _layer_spec(w_in, e),
                  pl.BlockSpec((TILE, LANES), lambda b, i: (i, 0)),
                  pl.BlockSpec((TILE, LANES), lambda b, i: (i, 0)),
                  pl.BlockSpec((TILE, LANES), lambda b, i: (i, 0)),
                  _const_spec(qg.shape), _const_spec(kg.shape), _const_spec(gsum.shape),
                  _const_spec(msel.shape)],
        out_specs=[pl.BlockSpec((None, TILE, Q_COLS), lambda b, i: (b, i, 0)),
                   pl.BlockSpec((None, TILE, K_COLS), lambda b, i: (b, i, 0)),
                   pl.BlockSpec((None, VT_ROWS, TILE), lambda b, i: (b, 0, i)),
                   pl.BlockSpec((None, N_MAPS, TILE), lambda b, i: (b, 0, i)),
                   pl.BlockSpec((None, SUBLANES, K_COLS), lambda b, i: (b, 0, 0))],
        out_shape=[jax.ShapeDtypeStruct((bsz, t_all, Q_COLS), BF16),
                   jax.ShapeDtypeStruct((bsz, t_all, K_COLS), BF16),
                   jax.ShapeDtypeStruct((bsz, VT_ROWS, t_all), BF16),
                   jax.ShapeDtypeStruct((bsz, N_MAPS, t_all), F32),
                   jax.ShapeDtypeStruct((bsz, SUBLANES, K_COLS), F32)],
        compiler_params=_params("arbitrary", "arbitrary"),
    )(*xs, mod, w_in, cos_t, sup_t, sdn_t, qg, kg, gsum, msel)


def _key_chunks(n_keys):
    sizes = ([n_keys % KEY_CHUNK] if n_keys % KEY_CHUNK else []) + [KEY_CHUNK] * (n_keys // KEY_CHUNK)
    starts = np.cumsum([0] + sizes[:-1])
    return [(int(a), int(n)) for a, n in zip(starts, sizes)]


def _attn_kernel(q_ref, k_ref, vt_ref, qn_ref, kn_ref, lamv_ref, subg_ref, o_ref, ot_ref, mx_ref, *,
                 lam_init):
    chunks = _key_chunks(k_ref.shape[0])
    lower = lax.broadcasted_iota(jnp.int32, (TILE, LANES), 1) < HEAD_DIM
    zero = jnp.zeros((TILE, LANES), BF16)

    lv = lamv_ref[...]
    lam = (jnp.exp(jnp.sum(lv[0:1] * lv[1:2], axis=-1, keepdims=True))
           - jnp.exp(jnp.sum(lv[2:3] * lv[3:4], axis=-1, keepdims=True)) + lam_init)
    subg = subg_ref[...]

    q0 = A_HEADS * LANES
    pairs = [(slice(hd * LANES, (hd + 1) * LANES), slice(hd * LANES, (hd + 1) * LANES),
              VT_BLOCKS[hd], VT_BLOCKS[hd]) for hd in range(A_HEADS)]
    pairs += [(slice(q0 + j * LANES, q0 + (j + 1) * LANES), slice(q0, q0 + LANES),
               VT_BLOCKS[A_HEADS], VT_BLOCKS[A_HEADS + 1]) for j in range(B_HEADS // 2)]
    n_pairs = len(pairs)

    def masked_q(i):
        qt = q_ref[:, pairs[i][0]]
        return jnp.concatenate([jnp.where(lower, qt, zero), jnp.where(lower, zero, qt)], axis=0)

    def estimate(i):
        k_lo = pairs[i][1].start
        halves = [jnp.sqrt(qn_ref[2 * i + half:2 * i + half + 1, :]
                           * kn_ref[0:1, k_lo + half * HEAD_DIM:k_lo + half * HEAD_DIM + 1])
                  for half in range(2)]
        return jnp.concatenate(halves, axis=1) * SHIFT_MARGIN

    def score_chunk(i, qcat, start, size):
        return lax.dot_general(k_ref[start:start + size, pairs[i][1]], qcat, (((1,), (1,)), ((), ())),
                               preferred_element_type=F32)

    def exp_chunk(s, shift, m8):
        p = jnp.exp2(s - shift).astype(BF16)
        return p, jnp.maximum(m8, jnp.max(s.reshape(s.shape[0] // SUBLANES, SUBLANES, 2 * TILE), axis=0))

    def value_chunk(i, start, size, p, accs):
        _, _, blk_lo, blk_hi = pairs[i]
        if blk_lo == blk_hi:
            base, n = blk_lo
            both = jnp.dot(vt_ref[base:base + n + ONES_ROWS, start:start + size], p, preferred_element_type=F32)
            parts = [both[:, :TILE], both[:, TILE:]]
        else:
            parts = [jnp.dot(vt_ref[base:base + n + ONES_ROWS, start:start + size],
                             p[:, half * TILE:(half + 1) * TILE],
                             preferred_element_type=F32)
                     for half, (base, n) in enumerate((blk_lo, blk_hi))]
        return parts if accs is None else [a + b for a, b in zip(accs, parts)]

    def finish(i, accs):
        dens = [acc[n:n + 1] for (_, n), acc in zip(pairs[i][2:], accs)]
        o_lo, o_hi = [acc[:n] * (1.0 / den) for (_, n), acc, den in zip(pairs[i][2:], accs, dens)]
        if i < A_HEADS:
            o = o_lo - lam * o_hi
            ms = jnp.mean(o * o, axis=0, keepdims=True)
            ot_ref[i * A_VDIM:(i + 1) * A_VDIM, :] = o * lax.rsqrt(ms + EPS) * subg * (1.0 - lam_init)
        else:
            base = q0 + (i - A_HEADS) * LANES
            ot_ref[base:base + HEAD_DIM, :] = o_lo
            ot_ref[base + HEAD_DIM:base + LANES, :] = o_hi
        return dens

    mx_ref[...] = jnp.zeros_like(mx_ref)

    def attempt(carry):
        n_done, _ = carry
        den_min, den_max = None, None
        items = [(t, start, size) for t in range(n_pairs) for start, size in chunks]
        qcat = masked_q(0)
        s_next = score_chunk(0, qcat, *items[0][1:])
        for idx, (t, start, size) in enumerate(items):
            if start == chunks[0][0]:
                exact = jnp.max(mx_ref[t], axis=0, keepdims=True)
                shift = jnp.where(n_done == 0, estimate(t), exact)
                m8 = jnp.full((SUBLANES, 2 * TILE), NEG_BIG, F32)
                accs = None
            p, m8 = exp_chunk(s_next, shift, m8)
            if idx + 1 < len(items):
                t_n, start_n, size_n = items[idx + 1]
                if t_n != t:
                    qcat = masked_q(t_n)
                s_next = score_chunk(t_n, qcat, start_n, size_n)
            accs = value_chunk(t, start, size, p, accs)
            if start == chunks[-1][0]:
                mx_ref[t] = m8
                for den in finish(t, accs):
                    den_min = den if den_min is None else jnp.minimum(den_min, den)
                    den_max = den if den_max is None else jnp.maximum(den_max, den)
        safe = jnp.logical_and(jnp.min(den_min) >= DEN_SAFE_MIN, jnp.max(den_max) <= DEN_SAFE_MAX)
        return n_done + 1, safe

    def again(carry):
        n_done, safe = carry
        return jnp.logical_or(n_done == 0, jnp.logical_and(n_done == 1, jnp.logical_not(safe)))

    lax.while_loop(again, attempt, (jnp.int32(0), jnp.bool_(False)))
    o_ref[...] = ot_ref[...].T.astype(BF16)


def _attention(q_all, k_all, vt_all, qn_all, kn_all, lam_vecs, subg, lam_init, context):
    bsz, t_all, _ = q_all.shape
    n_keys = CTX_LEN if context else t_all
    nq = 1 if context else (t_all - CTX_LEN) // TILE
    q_off = 0 if context else CTX_LEN // TILE
    kern = functools.partial(_attn_kernel, lam_init=lam_init)
    per_batch = dict(pipeline_mode=pl.Buffered(1))
    return pl.pallas_call(
        kern,
        grid=(bsz, nq),
        in_specs=[pl.BlockSpec((None, TILE, Q_COLS), lambda b, i: (b, i + q_off, 0)),
                  pl.BlockSpec((None, n_keys, K_COLS), lambda b, i: (b, 0, 0), **per_batch),
                  pl.BlockSpec((None, VT_ROWS, n_keys), lambda b, i: (b, 0, 0), **per_batch),
                  pl.BlockSpec((None, N_MAPS, TILE), lambda b, i: (b, 0, i + q_off)),
                  pl.BlockSpec((None, SUBLANES, K_COLS), lambda b, i: (b, 0, 0)),
                  _const_spec(lam_vecs.shape), _const_spec(subg.shape)],
        out_specs=pl.BlockSpec((None, TILE, Q_COLS), lambda b, i: (b, i, 0)),
        out_shape=jax.ShapeDtypeStruct((bsz, nq * TILE, Q_COLS), BF16),
        scratch_shapes=[pltpu.VMEM((Q_COLS, TILE), F32),
                        pltpu.VMEM((N_MAPS // 2, SUBLANES, 2 * TILE), F32)],
        compiler_params=_params("arbitrary", "arbitrary"),
    )(q_all, k_all, vt_all, qn_all, kn_all, lam_vecs, subg)


def _post_kernel(o_ref, x_ref, mod_ref, ln_ref, wo_ref, w1_ref, w2_ref, out_ref, *, alpha):
    n = o_ref.shape[0] // TILE
    rows = [slice(k * TILE, (k + 1) * TILE) for k in range(n)]

    def write(k, v):
        out_ref[rows[k], :] = v

    y_fns = [lambda r=r: jnp.dot(o_ref[r, :], wo_ref[...], preferred_element_type=F32) for r in rows]
    _post_tiles([x_ref[r, :] for r in rows], y_fns, mod_ref, ln_ref, w1_ref, w2_ref, alpha, write)


def _even_post(o, x, mod, ln, l, w_out, e, w1, w2, n_batch, kind, alpha):
    bsz, t, d = x.shape
    rows = TILE * _tiles_per_step(t // TILE)
    blk = pl.BlockSpec((None, rows, d), lambda b, i: (b, i, 0))
    return pl.pallas_call(
        functools.partial(_post_kernel, alpha=alpha),
        grid=(bsz, t // rows),
        in_specs=[blk, blk,
                  pl.BlockSpec((None, None, N_MOD, d), _mod_row_map(l, n_batch, kind)),
                  _layer_spec(ln, l), _layer_spec(w_out, e), _layer_spec(w1, l), _layer_spec(w2, l)],
        out_specs=blk,
        out_shape=jax.ShapeDtypeStruct((bsz, t, d), F32),
        compiler_params=_params("arbitrary", "arbitrary"),
    )(o, x, mod, ln, w_out, w1, w2)


def _odd_kernel(x_ref, xp_ref, xn_ref, mod_ref, ln_ref, wi_ref, cw_ref, cb_ref, wo_ref, w1_ref, w2_ref,
                out_ref, u_ref, *, alpha):
    n_rows = x_ref.shape[0]
    rows = [slice(k * TILE, (k + 1) * TILE) for k in range(n_rows // TILE)]
    i = pl.program_id(1)
    first = i == 0
    last = i == pl.num_programs(1) - 1
    x = x_ref[...]
    shift, scale1 = mod_ref[0:1, :], 1.0 + mod_ref[1:2, :]
    x_ext = jnp.concatenate([xp_ref[...], x, xn_ref[...]], axis=0)
    h_ext = (x_ext * scale1 + shift).astype(BF16)
    z_c = jnp.dot(h_ext, wi_ref[:, D_MODEL:2 * D_MODEL], preferred_element_type=F32)
    z_x = jnp.dot(h_ext, wi_ref[:, 2 * D_MODEL:3 * D_MODEL], preferred_element_type=F32)
    row = lax.broadcasted_iota(jnp.int32, (n_rows + 2 * HALO, 1), 0)
    live_from = jnp.where(first, HALO, 0)
    live_to = jnp.where(last, HALO + n_rows, n_rows + 2 * HALO)
    live = jnp.logical_and(row >= live_from, row < live_to)
    u_ref[...] = jnp.where(live, z_c * z_x, 0.0)

    def mixer(r):
        lo = HALO + r.start
        conv = (u_ref[lo - 1:lo - 1 + TILE, :] * cw_ref[0:1, :] + u_ref[lo:lo + TILE, :] * cw_ref[1:2, :]
                + u_ref[lo + 1:lo + 1 + TILE, :] * cw_ref[2:3, :] + cb_ref[...])
        h = (x_ref[r, :] * scale1 + shift).astype(BF16)
        z_b = jnp.dot(h, wi_ref[:, 0:D_MODEL], preferred_element_type=F32)
        return jnp.dot((z_b * conv).astype(BF16), wo_ref[...], preferred_element_type=F32)

    def write(k, v):
        out_ref[rows[k], :] = v

    _post_tiles([x_ref[r, :] for r in rows], [lambda r=r: mixer(r) for r in rows],
                mod_ref, ln_ref, w1_ref, w2_ref, alpha, write)


def _odd_layer(x, mod, ln, l, w_in, conv_w, conv_b, w_out, od, w1, w2, n_batch, kind, alpha):
    bsz, t, d = x.shape
    rows = TILE * _tiles_per_step(t // TILE)
    per_step = rows // HALO
    n_halo_blocks = t // HALO
    blk = pl.BlockSpec((None, rows, d), lambda b, i: (b, i, 0))
    return pl.pallas_call(
        functools.partial(_odd_kernel, alpha=alpha),
        grid=(bsz, t // rows),
        in_specs=[blk,
                  pl.BlockSpec((None, HALO, d), lambda b, i: (b, jnp.maximum(i * per_step - 1, 0), 0)),
                  pl.BlockSpec((None, HALO, d),
                               lambda b, i: (b, jnp.minimum((i + 1) * per_step, n_halo_blocks - 1), 0)),
                  pl.BlockSpec((None, None, N_MOD, d), _mod_row_map(l, n_batch, kind)),
                  _layer_spec(ln, l), _layer_spec(w_in, od), _layer_spec(conv_w, od),
                  _layer_spec(conv_b, od), _layer_spec(w_out, od),
                  _layer_spec(w1, l), _layer_spec(w2, l)],
        out_specs=blk,
        out_shape=jax.ShapeDtypeStruct((bsz, t, d), F32),
        scratch_shapes=[pltpu.VMEM((rows + 2 * HALO, d), F32)],
        compiler_params=_params("arbitrary", "arbitrary"),
    )(x, x, x, mod, ln, w_in, conv_w, conv_b, w_out, w1, w2)


def _rope_tables(n_lat):
    t = jnp.arange(n_lat)
    pos = jnp.stack([(t // GRID_W).astype(F32), (t % GRID_W).astype(F32)], axis=-1)
    inv_freq = ROPE_THETA ** (-jnp.arange(ROPE_FREQS, dtype=F32) / ROPE_FREQS)
    ang = pos[:, :, None] * inv_freq
    cos = jnp.repeat(jnp.cos(ang), 2, axis=1).reshape(n_lat, HEAD_DIM)
    sin = jnp.repeat(jnp.sin(ang), 2, axis=1).reshape(n_lat, HEAD_DIM)
    first_half = (np.arange(HEAD_DIM) % (2 * ROPE_FREQS)) < ROPE_FREQS
    s_up = jnp.where(first_half, -sin, 0.0)
    s_dn = jnp.where(first_half, 0.0, sin)
    reps = LANES // HEAD_DIM
    pad = lambda a, fill: jnp.concatenate(
        [jnp.full((CTX_LEN, LANES), fill, F32), jnp.tile(a, (1, reps))], axis=0)
    return pad(cos, 1.0), pad(s_up, 0.0), pad(s_dn, 0.0)


def _gqa_head_order():
    half = B_HEADS // 2
    return [h for j in range(half) for h in (j, j + half)]


def kernel(x, c, ctx, c_ctx, ada_w, ada_b, attn_w_in, attn_w_out, diff_lambda, diff_subln_g, q_norm_g, k_norm_g, conv_w_in, conv_w, conv_b, conv_w_out, mlp_w1, mlp_w2, ln_g, ln_b):
    bsz, n_lat, d = x.shape
    depth = ada_w.shape[0]
    assert d == D_MODEL and ctx.shape[1] == CTX_LEN == TILE and n_lat % TILE == 0
    alpha = (2 * depth) ** 0.25

    n_rows = -(-(bsz + 1) // SUBLANES) * SUBLANES
    c_all = jnp.concatenate([c, c_ctx[None, :], jnp.zeros((n_rows - bsz - 1, d), F32)], axis=0)
    mod = _ada_modulation(c_all, ada_w, ada_b).reshape(depth, n_rows, N_MOD, d)

    cos_t, sup_t, sdn_t = _rope_tables(n_lat)
    order = _gqa_head_order()
    bq_segs = [(1536 + HEAD_DIM * h, 1536 + HEAD_DIM * (h + 1)) for h in order]
    in_segs = [(0, 512)] + bq_segs + [(512, 1024), (2048, 2176), (1024, 1536), (2176, 2304)]
    out_segs = [(0, 512)] + [(512 + HEAD_DIM * h, 512 + HEAD_DIM * (h + 1)) for h in order]
    gsum = jnp.asarray(np.kron(np.eye(LANES // HEAD_DIM), np.ones((HEAD_DIM, HEAD_DIM))), BF16)
    msel = jnp.asarray(np.kron(np.eye(N_MAPS), np.ones((1, HEAD_DIM))), BF16)
    ln = jnp.stack([ln_g[:, 0], ln_b[:, 0], ln_g[:, 1], ln_b[:, 1]], axis=1)

    w1_all, w2_all = mlp_w1.astype(BF16), mlp_w2.astype(BF16)
    attn_in = jnp.concatenate([attn_w_in[:, :, a:b] for a, b in in_segs], axis=2).astype(BF16)
    attn_out = jnp.concatenate([attn_w_out[:, a:b, :] for a, b in out_segs], axis=1).astype(BF16)
    conv_in, conv_out = conv_w_in.astype(BF16), conv_w_out.astype(BF16)
    conv_bias = conv_b[:, None, :]
    tile2 = lambda g: jnp.tile(g, LANES // HEAD_DIM)[None, :]

    x_ctx, x_lat = ctx, x
    for l in range(depth):
        ctx_out = any(j % 2 == 0 for j in range(l + 1, depth))
        if l % 2 == 0:
            e = l // 2
            lam_init = 0.8 - 0.6 * math.exp(-0.3 * l)
            q_all, k_all, vt_all, qn_all, kn_all = _even_projection(
                (x_ctx, x_lat), mod, l, attn_in, e, cos_t, sup_t, sdn_t,
                tile2(q_norm_g[e]), tile2(k_norm_g[e]), gsum, msel, bsz)
            attend = functools.partial(_attention, q_all, k_all, vt_all, qn_all, kn_all, diff_lambda[e],
                                       diff_subln_g[e][:, None], lam_init)
            post = functools.partial(_even_post, mod=mod, ln=ln, l=l, w_out=attn_out, e=e, w1=w1_all,
                                     w2=w2_all, n_batch=bsz, alpha=alpha)
            x_lat = post(attend(False), x_lat, kind="latent")
            x_ctx = post(attend(True), x_ctx, kind="context") if ctx_out else None
        else:
            od = l // 2
            odd = functools.partial(_odd_layer, mod=mod, ln=ln, l=l, w_in=conv_in, conv_w=conv_w,
                                    conv_b=conv_bias, w_out=conv_out, od=od, w1=w1_all, w2=w2_all,
                                    n_batch=bsz, alpha=alpha)
            x_lat = odd(x_lat, kind="latent")
            x_ctx = odd(x_ctx, kind="context") if ctx_out else None
    return x_lat
```

```python
import functools
import math

import numpy as np
import jax
import jax.numpy as jnp
from jax import lax
from jax.experimental import pallas as pl
from jax.experimental.pallas import tpu as pltpu

F32 = jnp.float32
BF16 = jnp.bfloat16

D_MODEL = 1024
CTX_LEN = 256
GRID_W = 64
HEAD_DIM = 64
A_HEADS = 4
A_VDIM = 2 * HEAD_DIM
B_HEADS = 8
B_KV_HEADS = 2
Q_COLS = 1024
K_COLS = 640
V_COLS = 640
ONES_ROWS = 16
MLP_HIDDEN = 4 * D_MODEL
ROPE_THETA = 10000.0
ROPE_FREQS = HEAD_DIM // 4
N_MOD = 6
EPS = 1e-6

TILE = 256
LANES = 128
SUBLANES = 8
HALO = SUBLANES
HID_CHUNK = 1024
KEY_CHUNK = 512
N_MAPS = 2 * A_HEADS + B_HEADS
SHIFT_MARGIN = 1.03
DEN_SAFE_MIN = 2.0 ** -60
DEN_SAFE_MAX = 2.0 ** 60
NEG_BIG = -1e30
VMEM_LIMIT = 56 * 1024 * 1024


def _const_spec(shape):
    nd = len(shape)
    return pl.BlockSpec(shape, lambda *_: (0,) * nd, pipeline_mode=pl.Buffered(1))


def _layer_spec(stacked, l):
    nd = stacked.ndim
    return pl.BlockSpec((None,) + stacked.shape[1:], lambda *_: (l,) + (0,) * (nd - 1),
                        pipeline_mode=pl.Buffered(1))


def _stream_specs(x_ctx, x_lat):
    d = x_ctx.shape[-1]
    n_ctx = CTX_LEN // TILE
    return [pl.BlockSpec((None, TILE, d), lambda b, i: (b, 0, 0)),
            pl.BlockSpec((None, TILE, d), lambda b, i: (b, jnp.maximum(i - n_ctx, 0), 0))]


def _read_stream(ctx_ref, lat_ref):
    return jnp.where(pl.program_id(1) == 0, ctx_ref[...], lat_ref[...])


def _params(*sem):
    return pltpu.CompilerParams(dimension_semantics=sem, vmem_limit_bytes=VMEM_LIMIT)


def _ada_kernel(c_ref, w_ref, b_ref, o_ref):
    c = c_ref[...]
    act = c * (1.0 / (1.0 + jnp.exp(-c)))
    o_ref[...] = jnp.dot(act.astype(BF16), w_ref[...].astype(BF16),
                         preferred_element_type=F32) + b_ref[...]


def _ada_modulation(c_all, ada_w, ada_b):
    depth, d, n = ada_w.shape
    rows = c_all.shape[0]
    tn = 1536
    return pl.pallas_call(
        _ada_kernel,
        grid=(depth, n // tn),
        in_specs=[pl.BlockSpec((rows, d), lambda l, j: (0, 0)),
                  pl.BlockSpec((None, d, tn), lambda l, j: (l, 0, j)),
                  pl.BlockSpec((None, 1, tn), lambda l, j: (l, 0, j))],
        out_specs=pl.BlockSpec((None, rows, tn), lambda l, j: (l, 0, j)),
        out_shape=jax.ShapeDtypeStruct((depth, rows, n), F32),
        compiler_params=_params("arbitrary", "arbitrary"),
    )(c_all, ada_w, ada_b.reshape(depth, 1, n))


def _layer_norm(v, g, b):
    mu = jnp.mean(v, axis=-1, keepdims=True)
    d = v - mu
    var = jnp.mean(d * d, axis=-1, keepdims=True)
    return d * lax.rsqrt(var + EPS) * g + b


def _post_tiles(xs, y_fns, mod_ref, ln_ref, w1_ref, w2_ref, alpha, write):
    n_c = MLP_HIDDEN // HID_CHUNK

    def ln1(x, y):
        x1 = _layer_norm(alpha * x + mod_ref[2:3, :] * y, ln_ref[0:1, :], ln_ref[1:2, :])
        return x1, (x1 * (1.0 + mod_ref[4:5, :]) + mod_ref[3:4, :]).astype(BF16)

    def mlp(h2, chunks, y2):
        for c in chunks:
            sl = slice(c * HID_CHUNK, (c + 1) * HID_CHUNK)
            hid = jnp.maximum(jnp.dot(h2, w1_ref[:, sl], preferred_element_type=F32), 0.0)
            part = jnp.dot((hid * hid).astype(BF16), w2_ref[sl, :], preferred_element_type=F32)
            y2 = part if y2 is None else y2 + part
        return y2

    def ln2(x1, y2):
        return _layer_norm(alpha * x1 + mod_ref[5:6, :] * y2, ln_ref[2:3, :], ln_ref[3:4, :])

    x1a, h2a = ln1(xs[0], y_fns[0]())
    if len(xs) == 1:
        write(0, ln2(x1a, mlp(h2a, range(n_c), None)))
        return
    yb = y_fns[1]()
    y2a = mlp(h2a, range(0, n_c // 2), None)
    x1b, h2b = ln1(xs[1], yb)
    y2a = mlp(h2a, range(n_c // 2, n_c), y2a)
    y2b = mlp(h2b, range(0, n_c // 2), None)
    write(0, ln2(x1a, y2a))
    y2b = mlp(h2b, range(n_c // 2, n_c), y2b)
    write(1, ln2(x1b, y2b))


def _tiles_per_step(n_tiles):
    return 2 if n_tiles % 2 == 0 else 1


def _mod_row_map(l, n_batch, kind):
    if kind == "both":
        return lambda b, i: (l, jnp.where(i == 0, n_batch, b), 0, 0)
    if kind == "context":
        return lambda b, i: (l, n_batch, 0, 0)
    return lambda b, i: (l, b, 0, 0)


def _vt_row_layout():
    blocks, base = [], 0
    for n in [A_VDIM] * A_HEADS + [HEAD_DIM] * B_KV_HEADS:
        blocks.append((base, n))
        base += n + ONES_ROWS
    return blocks, base


VT_BLOCKS, VT_ROWS = _vt_row_layout()


def _rope(z, cos, s_up, s_dn):
    n = z.shape[-1]
    return z * cos + pltpu.roll(z, n - ROPE_FREQS, 1) * s_up + pltpu.roll(z, ROPE_FREQS, 1) * s_dn


def _group_mean_sq(z, gsum_ref):
    z2 = z * z
    hi = z2.astype(BF16)
    lo = (z2 - hi.astype(F32)).astype(BF16)
    g = gsum_ref[...]
    ss = jnp.dot(hi, g, preferred_element_type=F32) + jnp.dot(lo, g, preferred_element_type=F32)
    return ss * (1.0 / HEAD_DIM)


def _proj_kernel(xc_ref, xl_ref, mod_ref, w_ref, cos_ref, sup_ref, sdn_ref, qg_ref, kg_ref, gsum_ref, msel_ref,
                 q_ref, k_ref, vt_ref, qn_ref, kn_ref):
    h = (_read_stream(xc_ref, xl_ref) * (1.0 + mod_ref[1:2, :]) + mod_ref[0:1, :]).astype(BF16)
    cos, s_up, s_dn = cos_ref[...], sup_ref[...], sdn_ref[...]
    scale = HEAD_DIM ** -0.5 * math.log2(math.e)

    def lane_groups(z, fn):
        return [fn(z[:, g * LANES:(g + 1) * LANES]) for g in range(z.shape[1] // LANES)]

    def rope_all(z, mul):
        return jnp.concatenate(lane_groups(z, lambda t: _rope(t, cos, s_up, s_dn) * mul), axis=1)

    def norm_all(z, gain_ref):
        gain = gain_ref[...]
        return jnp.concatenate(
            lane_groups(z, lambda t: t * lax.rsqrt(_group_mean_sq(t, gsum_ref) + EPS) * gain), axis=1)

    aq = rope_all(jnp.dot(h, w_ref[:, 0:512], preferred_element_type=F32), scale)
    q_ref[:, 0:512] = aq.astype(BF16)
    bq = rope_all(norm_all(jnp.dot(h, w_ref[:, 512:1024], preferred_element_type=F32), qg_ref), scale)
    q_ref[:, 512:1024] = bq.astype(BF16)
    ak = rope_all(jnp.dot(h, w_ref[:, 1024:1536], preferred_element_type=F32), 1.0)
    k_ref[:, 0:512] = ak.astype(BF16)
    bk = rope_all(norm_all(jnp.dot(h, w_ref[:, 1536:1664], preferred_element_type=F32), kg_ref), 1.0)
    k_ref[:, 512:640] = bk.astype(BF16)

    q_sq = jnp.concatenate([aq * aq, bq * bq], axis=1).astype(BF16)
    qn_ref[...] = lax.dot_general(msel_ref[...], q_sq, (((1,), (1,)), ((), ())), preferred_element_type=F32)
    k_sq = jnp.concatenate([ak * ak, bk * bk], axis=1).astype(BF16)
    g = gsum_ref[...]
    k_n2 = jnp.concatenate(lane_groups(k_sq, lambda t: jnp.dot(t, g, preferred_element_type=F32)), axis=1)
    k_max = jnp.broadcast_to(jnp.max(k_n2, axis=0, keepdims=True), kn_ref.shape)

    @pl.when(pl.program_id(1) == 0)
    def _():
        kn_ref[...] = k_max

    @pl.when(pl.program_id(1) != 0)
    def _():
        kn_ref[...] = jnp.maximum(kn_ref[...], k_max)

    v = jnp.dot(h, w_ref[:, 1664:2304], preferred_element_type=F32)
    vt = v.T
    ones = jnp.ones((ONES_ROWS, TILE), F32)
    pieces, col = [], 0
    for _, n in VT_BLOCKS:
        pieces += [vt[col:col + n], ones]
        col += n
    vt_ref[...] = jnp.concatenate(pieces, axis=0).astype(BF16)


def _even_projection(xs, mod, l, w_in, e, cos_t, sup_t, sdn_t, qg, kg, gsum, msel, n_batch):
    x_specs = _stream_specs(*xs)
    bsz, d = xs[0].shape[0], xs[0].shape[-1]
    t_all = xs[0].shape[1] + xs[1].shape[1]
    nt = t_all // TILE
    return pl.pallas_call(
        _proj_kernel,
        grid=(bsz, nt),
        in_specs=x_specs + [
                  pl.BlockSpec((None, None, N_MOD, d), _mod_row_map(l, n_batch, "both")),
                  _layer_spec(w_in, e),
                  pl.BlockSpec((TILE, LANES), lambda b, i: (i, 0)),
                  pl.BlockSpec((TILE, LANES), lambda b, i: (i, 0)),
                  pl.BlockSpec((TILE, LANES), lambda b, i: (i, 0)),
                  _const_spec(qg.shape), _const_spec(kg.shape), _const_spec(gsum.shape),
                  _const_spec(msel.shape)],
        out_specs=[pl.BlockSpec((None, TILE, Q_COLS), lambda b, i: (b, i, 0)),
                   pl.BlockSpec((None, TILE, K_COLS), lambda b, i: (b, i, 0)),
                   pl.BlockSpec((None, VT_ROWS, TILE), lambda b, i: (b, 0, i)),
                   pl.BlockSpec((None, N_MAPS, TILE), lambda b, i: (b, 0, i)),
                   pl.BlockSpec((None, SUBLANES, K_COLS), lambda b, i: (b, 0, 0))],
        out_shape=[jax.ShapeDtypeStruct((bsz, t_all, Q_COLS), BF16),
                   jax.ShapeDtypeStruct((bsz, t_all, K_COLS), BF16),
                   jax.ShapeDtypeStruct((bsz, VT_ROWS, t_all), BF16),
                   jax.ShapeDtypeStruct((bsz, N_MAPS, t_all), F32),
                   jax.ShapeDtypeStruct((bsz, SUBLANES, K_COLS), F32)],
        compiler_params=_params("arbitrary", "arbitrary"),
    )(*xs, mod, w_in, cos_t, sup_t, sdn_t, qg, kg, gsum, msel)


def _key_chunks(n_keys):
    sizes = ([n_keys % KEY_CHUNK] if n_keys % KEY_CHUNK else []) + [KEY_CHUNK] * (n_keys // KEY_CHUNK)
    starts = np.cumsum([0] + sizes[:-1])
    return [(int(a), int(n)) for a, n in zip(starts, sizes)]


def _attn_kernel(q_ref, k_ref, vt_ref, qn_ref, kn_ref, lamv_ref, subg_ref, o_ref, ot_ref, p0_ref, p1_ref,
                 mx_ref, *, lam_init):
    p_refs = (p0_ref, p1_ref)
    chunks = _key_chunks(k_ref.shape[0])
    upper_rows = lax.broadcasted_iota(jnp.int32, (LANES, TILE), 0) >= HEAD_DIM
    zero = jnp.zeros((LANES, TILE), BF16)

    lv = lamv_ref[...]
    lam = (jnp.exp(jnp.sum(lv[0:1] * lv[1:2], axis=-1, keepdims=True))
           - jnp.exp(jnp.sum(lv[2:3] * lv[3:4], axis=-1, keepdims=True)) + lam_init)
    subg = subg_ref[...]

    q0 = A_HEADS * LANES
    pairs = [(slice(hd * LANES, (hd + 1) * LANES), slice(hd * LANES, (hd + 1) * LANES),
              VT_BLOCKS[hd], VT_BLOCKS[hd]) for hd in range(A_HEADS)]
    pairs += [(slice(q0 + j * LANES, q0 + (j + 1) * LANES), slice(q0, q0 + LANES),
               VT_BLOCKS[A_HEADS], VT_BLOCKS[A_HEADS + 1]) for j in range(B_HEADS // 2)]
    n_pairs = len(pairs)

    q_t = q_ref[...].astype(F32).T

    def masked_q(i):
        qt = q_t[pairs[i][0], :].astype(BF16)
        return jnp.concatenate([jnp.where(upper_rows, zero, qt), jnp.where(upper_rows, qt, zero)], axis=1)

    def estimate(i):
        k_lo = pairs[i][1].start
        halves = [jnp.sqrt(qn_ref[2 * i + half:2 * i + half + 1, :]
                           * kn_ref[0:1, k_lo + half * HEAD_DIM:k_lo + half * HEAD_DIM + 1])
                  for half in range(2)]
        return jnp.concatenate(halves, axis=1) * SHIFT_MARGIN

    def score_chunk(i, qcat, start, size, shift, m8):
        s = jnp.dot(k_ref[start:start + size, pairs[i][1]], qcat, preferred_element_type=F32)
        p_refs[i % 2][start:start + size, :] = jnp.exp2(s - shift).astype(BF16)
        return jnp.maximum(m8, jnp.max(s.reshape(size // SUBLANES, SUBLANES, 2 * TILE), axis=0))

    def value_chunk(i, start, size, accs):
        _, _, blk_lo, blk_hi = pairs[i]
        p_ref = p_refs[i % 2]
        if blk_lo == blk_hi:
            base, n = blk_lo
            both = jnp.dot(vt_ref[base:base + n + ONES_ROWS, start:start + size], p_ref[start:start + size, :],
                           preferred_element_type=F32)
            parts = [both[:, :TILE], both[:, TILE:]]
        else:
            parts = [jnp.dot(vt_ref[base:base + n + ONES_ROWS, start:start + size],
                             p_ref[start:start + size, half * TILE:(half + 1) * TILE],
                             preferred_element_type=F32)
                     for half, (base, n) in enumerate((blk_lo, blk_hi))]
        return parts if accs is None else [a + b for a, b in zip(accs, parts)]

    def finish(i, accs):
        dens = [acc[n:n + 1] for (_, n), acc in zip(pairs[i][2:], accs)]
        o_lo, o_hi = [acc[:n] * (1.0 / den) for (_, n), acc, den in zip(pairs[i][2:], accs, dens)]
        if i < A_HEADS:
            o = o_lo - lam * o_hi
            ms = jnp.mean(o * o, axis=0, keepdims=True)
            ot_ref[i * A_VDIM:(i + 1) * A_VDIM, :] = o * lax.rsqrt(ms + EPS) * subg * (1.0 - lam_init)
        else:
            base = q0 + (i - A_HEADS) * LANES
            ot_ref[base:base + HEAD_DIM, :] = o_lo
            ot_ref[base + HEAD_DIM:base + LANES, :] = o_hi
        return dens

    mx_ref[...] = jnp.zeros_like(mx_ref)

    def attempt(carry):
        n_done, _ = carry
        den_min, den_max = None, None
        for t in range(n_pairs + 1):
            if t < n_pairs:
                qcat = masked_q(t)
                exact = jnp.max(mx_ref[t], axis=0, keepdims=True)
                shift = jnp.where(n_done == 0, estimate(t), exact)
                m8 = jnp.full((SUBLANES, 2 * TILE), NEG_BIG, F32)
            accs = None
            for start, size in chunks:
                if t < n_pairs:
                    m8 = score_chunk(t, qcat, start, size, shift, m8)
                if 1 <= t:
                    accs = value_chunk(t - 1, start, size, accs)
            if t < n_pairs:
                mx_ref[t] = m8
            if 1 <= t:
                for den in finish(t - 1, accs):
                    den_min = den if den_min is None else jnp.minimum(den_min, den)
                    den_max = den if den_max is None else jnp.maximum(den_max, den)
        safe = jnp.logical_and(jnp.min(den_min) >= DEN_SAFE_MIN, jnp.max(den_max) <= DEN_SAFE_MAX)
        return n_done + 1, safe

    def again(carry):
        n_done, safe = carry
        return jnp.logical_or(n_done == 0, jnp.logical_and(n_done == 1, jnp.logical_not(safe)))

    lax.while_loop(again, attempt, (jnp.int32(0), jnp.bool_(False)))
    o_ref[...] = ot_ref[...].T.astype(BF16)


def _attention(q_all, k_all, vt_all, qn_all, kn_all, lam_vecs, subg, lam_init, context):
    bsz, t_all, _ = q_all.shape
    n_keys = CTX_LEN if context else t_all
    nq = 1 if context else (t_all - CTX_LEN) // TILE
    q_off = 0 if context else CTX_LEN // TILE
    kern = functools.partial(_attn_kernel, lam_init=lam_init)
    per_batch = dict(pipeline_mode=pl.Buffered(1))
    return pl.pallas_call(
        kern,
        grid=(bsz, nq),
        in_specs=[pl.BlockSpec((None, TILE, Q_COLS), lambda b, i: (b, i + q_off, 0)),
                  pl.BlockSpec((None, n_keys, K_COLS), lambda b, i: (b, 0, 0), **per_batch),
                  pl.BlockSpec((None, VT_ROWS, n_keys), lambda b, i: (b, 0, 0), **per_batch),
                  pl.BlockSpec((None, N_MAPS, TILE), lambda b, i: (b, 0, i + q_off)),
                  pl.BlockSpec((None, SUBLANES, K_COLS), lambda b, i: (b, 0, 0)),
                  _const_spec(lam_vecs.shape), _const_spec(subg.shape)],
        out_specs=pl.BlockSpec((None, TILE, Q_COLS), lambda b, i: (b, i, 0)),
        out_shape=jax.ShapeDtypeStruct((bsz, nq * TILE, Q_COLS), BF16),
        scratch_shapes=[pltpu.VMEM((Q_COLS, TILE), F32),
                        pltpu.VMEM((n_keys, 2 * TILE), BF16),
                        pltpu.VMEM((n_keys, 2 * TILE), BF16),
                        pltpu.VMEM((N_MAPS // 2, SUBLANES, 2 * TILE), F32)],
        compiler_params=_params("arbitrary", "arbitrary"),
    )(q_all, k_all, vt_all, qn_all, kn_all, lam_vecs, subg)


def _post_kernel(o_ref, x_ref, mod_ref, ln_ref, wo_ref, w1_ref, w2_ref, out_ref, *, alpha):
    n = o_ref.shape[0] // TILE
    rows = [slice(k * TILE, (k + 1) * TILE) for k in range(n)]

    def write(k, v):
        out_ref[rows[k], :] = v

    y_fns = [lambda r=r: jnp.dot(o_ref[r, :], wo_ref[...], preferred_element_type=F32) for r in rows]
    _post_tiles([x_ref[r, :] for r in rows], y_fns, mod_ref, ln_ref, w1_ref, w2_ref, alpha, write)


def _even_post(o, x, mod, ln, l, w_out, e, w1, w2, n_batch, kind, alpha):
    bsz, t, d = x.shape
    rows = TILE * _tiles_per_step(t // TILE)
    blk = pl.BlockSpec((None, rows, d), lambda b, i: (b, i, 0))
    return pl.pallas_call(
        functools.partial(_post_kernel, alpha=alpha),
        grid=(bsz, t // rows),
        in_specs=[blk, blk,
                  pl.BlockSpec((None, None, N_MOD, d), _mod_row_map(l, n_batch, kind)),
                  _layer_spec(ln, l), _layer_spec(w_out, e), _layer_spec(w1, l), _layer_spec(w2, l)],
        out_specs=blk,
        out_shape=jax.ShapeDtypeStruct((bsz, t, d), F32),
        compiler_params=_params("arbitrary", "arbitrary"),
    )(o, x, mod, ln, w_out, w1, w2)


def _odd_kernel(x_ref, xp_ref, xn_ref, mod_ref, ln_ref, wi_ref, cw_ref, cb_ref, wo_ref, w1_ref, w2_ref,
                out_ref, u_ref, *, alpha):
    n_rows = x_ref.shape[0]
    rows = [slice(k * TILE, (k + 1) * TILE) for k in range(n_rows // TILE)]
    i = pl.program_id(1)
    first = i == 0
    last = i == pl.num_programs(1) - 1
    x = x_ref[...]
    shift, scale1 = mod_ref[0:1, :], 1.0 + mod_ref[1:2, :]
    x_ext = jnp.concatenate([xp_ref[...], x, xn_ref[...]], axis=0)
    h_ext = (x_ext * scale1 + shift).astype(BF16)
    z_c = jnp.dot(h_ext, wi_ref[:, D_MODEL:2 * D_MODEL], preferred_element_type=F32)
    z_x = jnp.dot(h_ext, wi_ref[:, 2 * D_MODEL:3 * D_MODEL], preferred_element_type=F32)
    row = lax.broadcasted_iota(jnp.int32, (n_rows + 2 * HALO, 1), 0)
    live_from = jnp.where(first, HALO, 0)
    live_to = jnp.where(last, HALO + n_rows, n_rows + 2 * HALO)
    live = jnp.logical_and(row >= live_from, row < live_to)
    u_ref[...] = jnp.where(live, z_c * z_x, 0.0)

    def mixer(r):
        lo = HALO + r.start
        conv = (u_ref[lo - 1:lo - 1 + TILE, :] * cw_ref[0:1, :] + u_ref[lo:lo + TILE, :] * cw_ref[1:2, :]
                + u_ref[lo + 1:lo + 1 + TILE, :] * cw_ref[2:3, :] + cb_ref[...])
        h = (x_ref[r, :] * scale1 + shift).astype(BF16)
        z_b = jnp.dot(h, wi_ref[:, 0:D_MODEL], preferred_element_type=F32)
        return jnp.dot((z_b * conv).astype(BF16), wo_ref[...], preferred_element_type=F32)

    def write(k, v):
        out_ref[rows[k], :] = v

    _post_tiles([x_ref[r, :] for r in rows], [lambda r=r: mixer(r) for r in rows],
                mod_ref, ln_ref, w1_ref, w2_ref, alpha, write)


def _odd_layer(x, mod, ln, l, w_in, conv_w, conv_b, w_out, od, w1, w2, n_batch, kind, alpha):
    bsz, t, d = x.shape
    rows = TILE * _tiles_per_step(t // TILE)
    per_step = rows // HALO
    n_halo_blocks = t // HALO
    blk = pl.BlockSpec((None, rows, d), lambda b, i: (b, i, 0))
    return pl.pallas_call(
        functools.partial(_odd_kernel, alpha=alpha),
        grid=(bsz, t // rows),
        in_specs=[blk,
                  pl.BlockSpec((None, HALO, d), lambda b, i: (b, jnp.maximum(i * per_step - 1, 0), 0)),
                  pl.BlockSpec((None, HALO, d),
                               lambda b, i: (b, jnp.minimum((i + 1) * per_step, n_halo_blocks - 1), 0)),
                  pl.BlockSpec((None, None, N_MOD, d), _mod_row_map(l, n_batch, kind)),
                  _layer_spec(ln, l), _layer_spec(w_in, od), _layer_spec(conv_w, od),
                  _layer_spec(conv_b, od), _layer_spec(w_out, od),
                  _layer_spec(w1, l), _layer_spec(w2, l)],
        out_specs=blk,
        out_shape=jax.ShapeDtypeStruct((bsz, t, d), F32),
        scratch_shapes=[pltpu.VMEM((rows + 2 * HALO, d), F32)],
        compiler_params=_params("arbitrary", "arbitrary"),
    )(x, x, x, mod, ln, w_in, conv_w, conv_b, w_out, w1, w2)


def _rope_tables(n_lat):
    t = jnp.arange(n_lat)
    pos = jnp.stack([(t // GRID_W).astype(F32), (t % GRID_W).astype(F32)], axis=-1)
    inv_freq = ROPE_THETA ** (-jnp.arange(ROPE_FREQS, dtype=F32) / ROPE_FREQS)
    ang = pos[:, :, None] * inv_freq
    cos = jnp.repeat(jnp.cos(ang), 2, axis=1).reshape(n_lat, HEAD_DIM)
    sin = jnp.repeat(jnp.sin(ang), 2, axis=1).reshape(n_lat, HEAD_DIM)
    first_half = (np.arange(HEAD_DIM) % (2 * ROPE_FREQS)) < ROPE_FREQS
    s_up = jnp.where(first_half, -sin, 0.0)
    s_dn = jnp.where(first_half, 0.0, sin)
    reps = LANES // HEAD_DIM
    pad = lambda a, fill: jnp.concatenate(
        [jnp.full((CTX_LEN, LANES), fill, F32), jnp.tile(a, (1, reps))], axis=0)
    return pad(cos, 1.0), pad(s_up, 0.0), pad(s_dn, 0.0)


def _gqa_head_order():
    half = B_HEADS // 2
    return [h for j in range(half) for h in (j, j + half)]


def kernel(x, c, ctx, c_ctx, ada_w, ada_b, attn_w_in, attn_w_out, diff_lambda, diff_subln_g, q_norm_g, k_norm_g, conv_w_in, conv_w, conv_b, conv_w_out, mlp_w1, mlp_w2, ln_g, ln_b):
    bsz, n_lat, d = x.shape
    depth = ada_w.shape[0]
    assert d == D_MODEL and ctx.shape[1] == CTX_LEN == TILE and n_lat % TILE == 0
    alpha = (2 * depth) ** 0.25

    n_rows = -(-(bsz + 1) // SUBLANES) * SUBLANES
    c_all = jnp.concatenate([c, c_ctx[None, :], jnp.zeros((n_rows - bsz - 1, d), F32)], axis=0)
    mod = _ada_modulation(c_all, ada_w, ada_b).reshape(depth, n_rows, N_MOD, d)

    cos_t, sup_t, sdn_t = _rope_tables(n_lat)
    order = _gqa_head_order()
    bq_segs = [(1536 + HEAD_DIM * h, 1536 + HEAD_DIM * (h + 1)) for h in order]
    in_segs = [(0, 512)] + bq_segs + [(512, 1024), (2048, 2176), (1024, 1536), (2176, 2304)]
    out_segs = [(0, 512)] + [(512 + HEAD_DIM * h, 512 + HEAD_DIM * (h + 1)) for h in order]
    gsum = jnp.asarray(np.kron(np.eye(LANES // HEAD_DIM), np.ones((HEAD_DIM, HEAD_DIM))), BF16)
    msel = jnp.asarray(np.kron(np.eye(N_MAPS), np.ones((1, HEAD_DIM))), BF16)
    ln = jnp.stack([ln_g[:, 0], ln_b[:, 0], ln_g[:, 1], ln_b[:, 1]], axis=1)

    w1_all, w2_all = mlp_w1.astype(BF16), mlp_w2.astype(BF16)
    attn_in = jnp.concatenate([attn_w_in[:, :, a:b] for a, b in in_segs], axis=2).astype(BF16)
    attn_out = jnp.concatenate([attn_w_out[:, a:b, :] for a, b in out_segs], axis=1).astype(BF16)
    conv_in, conv_out = conv_w_in.astype(BF16), conv_w_out.astype(BF16)
    conv_bias = conv_b[:, None, :]
    tile2 = lambda g: jnp.tile(g, LANES // HEAD_DIM)[None, :]

    x_ctx, x_lat = ctx, x
    for l in range(depth):
        ctx_out = any(j % 2 == 0 for j in range(l + 1, depth))
        if l % 2 == 0:
            e = l // 2
            lam_init = 0.8 - 0.6 * math.exp(-0.3 * l)
            q_all, k_all, vt_all, qn_all, kn_all = _even_projection(
                (x_ctx, x_lat), mod, l, attn_in, e, cos_t, sup_t, sdn_t,
                tile2(q_norm_g[e]), tile2(k_norm_g[e]), gsum, msel, bsz)
            attend = functools.partial(_attention, q_all, k_all, vt_all, qn_all, kn_all, diff_lambda[e],
                                       diff_subln_g[e][:, None], lam_init)
            post = functools.partial(_even_post, mod=mod, ln=ln, l=l, w_out=attn_out, e=e, w1=w1_all,
                                     w2=w2_all, n_batch=bsz, alpha=alpha)
            x_lat = post(attend(False), x_lat, kind="latent")
            x_ctx = post(attend(True), x_ctx, kind="context") if ctx_out else None
        else:
            od = l // 2
            odd = functools.partial(_odd_layer, mod=mod, ln=ln, l=l, w_in=conv_in, conv_w=conv_w,
                                    conv_b=conv_bias, w_out=conv_out, od=od, w1=w1_all, w2=w2_all,
                                    n_batch=bsz, alpha=alpha)
            x_lat = odd(x_lat, kind="latent")
            x_ctx = odd(x_ctx, kind="context") if ctx_out else None
    return x_lat
```

```python
import functools
import math

import numpy as np
import jax
import jax.numpy as jnp
from jax import lax
from jax.experimental import pallas as pl
from jax.experimental.pallas import tpu as pltpu

F32 = jnp.float32
BF16 = jnp.bfloat16

D_MODEL = 1024
CTX_LEN = 256
GRID_W = 64
HEAD_DIM = 64
A_HEADS = 4
A_VDIM = 2 * HEAD_DIM
B_HEADS = 8
B_KV_HEADS = 2
Q_COLS = 1024
K_COLS = 640
V_COLS = 640
ONES_ROWS = 16
MLP_HIDDEN = 4 * D_MODEL
ROPE_THETA = 10000.0
ROPE_FREQS = HEAD_DIM // 4
N_MOD = 6
EPS = 1e-6

TILE = 256
LANES = 128
SUBLANES = 8
HALO = SUBLANES
HID_CHUNK = 1024
KEY_CHUNK = 1024
N_MAPS = 2 * A_HEADS + B_HEADS
SHIFT_MARGIN = 1.03
DEN_SAFE_MIN = 2.0 ** -60
DEN_SAFE_MAX = 2.0 ** 60
NEG_BIG = -1e30
VMEM_LIMIT = 56 * 1024 * 1024


def _const_spec(shape):
    nd = len(shape)
    return pl.BlockSpec(shape, lambda *_: (0,) * nd, pipeline_mode=pl.Buffered(1))


def _layer_spec(stacked, l):
    nd = stacked.ndim
    return pl.BlockSpec((None,) + stacked.shape[1:], lambda *_: (l,) + (0,) * (nd - 1),
                        pipeline_mode=pl.Buffered(1))


def _stream_specs(x_ctx, x_lat):
    d = x_ctx.shape[-1]
    n_ctx = CTX_LEN // TILE
    return [pl.BlockSpec((None, TILE, d), lambda b, i: (b, 0, 0)),
            pl.BlockSpec((None, TILE, d), lambda b, i: (b, jnp.maximum(i - n_ctx, 0), 0))]


def _read_stream(ctx_ref, lat_ref):
    return jnp.where(pl.program_id(1) == 0, ctx_ref[...], lat_ref[...])


def _params(*sem):
    return pltpu.CompilerParams(dimension_semantics=sem, vmem_limit_bytes=VMEM_LIMIT)


def _ada_kernel(c_ref, w_ref, b_ref, o_ref):
    c = c_ref[...]
    act = c * (1.0 / (1.0 + jnp.exp(-c)))
    o_ref[...] = jnp.dot(act.astype(BF16), w_ref[...].astype(BF16),
                         preferred_element_type=F32) + b_ref[...]


def _ada_modulation(c_all, ada_w, ada_b):
    depth, d, n = ada_w.shape
    rows = c_all.shape[0]
    tn = 1536
    return pl.pallas_call(
        _ada_kernel,
        grid=(depth, n // tn),
        in_specs=[pl.BlockSpec((rows, d), lambda l, j: (0, 0)),
                  pl.BlockSpec((None, d, tn), lambda l, j: (l, 0, j)),
                  pl.BlockSpec((None, 1, tn), lambda l, j: (l, 0, j))],
        out_specs=pl.BlockSpec((None, rows, tn), lambda l, j: (l, 0, j)),
        out_shape=jax.ShapeDtypeStruct((depth, rows, n), F32),
        compiler_params=_params("arbitrary", "arbitrary"),
    )(c_all, ada_w, ada_b.reshape(depth, 1, n))


def _layer_norm(v, g, b):
    mu = jnp.mean(v, axis=-1, keepdims=True)
    d = v - mu
    var = jnp.mean(d * d, axis=-1, keepdims=True)
    return d * lax.rsqrt(var + EPS) * g + b


def _post_tiles(xs, y_fns, mod_ref, ln_ref, w1_ref, w2_ref, alpha, write):
    n_c = MLP_HIDDEN // HID_CHUNK

    def ln1(x, y):
        x1 = _layer_norm(alpha * x + mod_ref[2:3, :] * y, ln_ref[0:1, :], ln_ref[1:2, :])
        return x1, (x1 * (1.0 + mod_ref[4:5, :]) + mod_ref[3:4, :]).astype(BF16)

    def mlp(h2, chunks, y2):
        for c in chunks:
            sl = slice(c * HID_CHUNK, (c + 1) * HID_CHUNK)
            hid = jnp.maximum(jnp.dot(h2, w1_ref[:, sl], preferred_element_type=F32), 0.0)
            part = jnp.dot((hid * hid).astype(BF16), w2_ref[sl, :], preferred_element_type=F32)
            y2 = part if y2 is None else y2 + part
        return y2

    def ln2(x1, y2):
        return _layer_norm(alpha * x1 + mod_ref[5:6, :] * y2, ln_ref[2:3, :], ln_ref[3:4, :])

    x1a, h2a = ln1(xs[0], y_fns[0]())
    if len(xs) == 1:
        write(0, ln2(x1a, mlp(h2a, range(n_c), None)))
        return
    yb = y_fns[1]()
    y2a = mlp(h2a, range(0, n_c // 2), None)
    x1b, h2b = ln1(xs[1], yb)
    y2a = mlp(h2a, range(n_c // 2, n_c), y2a)
    y2b = mlp(h2b, range(0, n_c // 2), None)
    write(0, ln2(x1a, y2a))
    y2b = mlp(h2b, range(n_c // 2, n_c), y2b)
    write(1, ln2(x1b, y2b))


def _tiles_per_step(n_tiles):
    return 2 if n_tiles % 2 == 0 else 1


def _mod_row_map(l, n_batch, kind):
    if kind == "both":
        return lambda b, i: (l, jnp.where(i == 0, n_batch, b), 0, 0)
    if kind == "context":
        return lambda b, i: (l, n_batch, 0, 0)
    return lambda b, i: (l, b, 0, 0)


def _vt_row_layout():
    blocks, base = [], 0
    for n in [A_VDIM] * A_HEADS + [HEAD_DIM] * B_KV_HEADS:
        blocks.append((base, n))
        base += n + ONES_ROWS
    return blocks, base


VT_BLOCKS, VT_ROWS = _vt_row_layout()


def _rope(z, cos, s_up, s_dn):
    n = z.shape[-1]
    return z * cos + pltpu.roll(z, n - ROPE_FREQS, 1) * s_up + pltpu.roll(z, ROPE_FREQS, 1) * s_dn


def _group_mean_sq(z, gsum_ref):
    z2 = z * z
    hi = z2.astype(BF16)
    lo = (z2 - hi.astype(F32)).astype(BF16)
    g = gsum_ref[...]
    ss = jnp.dot(hi, g, preferred_element_type=F32) + jnp.dot(lo, g, preferred_element_type=F32)
    return ss * (1.0 / HEAD_DIM)


def _proj_kernel(xc_ref, xl_ref, mod_ref, w_ref, cos_ref, sup_ref, sdn_ref, qg_ref, kg_ref, gsum_ref, msel_ref,
                 q_ref, k_ref, vt_ref, qn_ref, kn_ref):
    h = (_read_stream(xc_ref, xl_ref) * (1.0 + mod_ref[1:2, :]) + mod_ref[0:1, :]).astype(BF16)
    cos, s_up, s_dn = cos_ref[...], sup_ref[...], sdn_ref[...]
    scale = HEAD_DIM ** -0.5 * math.log2(math.e)

    def lane_groups(z, fn):
        return [fn(z[:, g * LANES:(g + 1) * LANES]) for g in range(z.shape[1] // LANES)]

    def rope_all(z, mul):
        return jnp.concatenate(lane_groups(z, lambda t: _rope(t, cos, s_up, s_dn) * mul), axis=1)

    def norm_all(z, gain_ref):
        gain = gain_ref[...]
        return jnp.concatenate(
            lane_groups(z, lambda t: t * lax.rsqrt(_group_mean_sq(t, gsum_ref) + EPS) * gain), axis=1)

    aq = rope_all(jnp.dot(h, w_ref[:, 0:512], preferred_element_type=F32), scale)
    q_ref[:, 0:512] = aq.astype(BF16)
    bq = rope_all(norm_all(jnp.dot(h, w_ref[:, 512:1024], preferred_element_type=F32), qg_ref), scale)
    q_ref[:, 512:1024] = bq.astype(BF16)
    ak = rope_all(jnp.dot(h, w_ref[:, 1024:1536], preferred_element_type=F32), 1.0)
    k_ref[:, 0:512] = ak.astype(BF16)
    bk = rope_all(norm_all(jnp.dot(h, w_ref[:, 1536:1664], preferred_element_type=F32), kg_ref), 1.0)
    k_ref[:, 512:640] = bk.astype(BF16)

    q_sq = jnp.concatenate([aq * aq, bq * bq], axis=1).astype(BF16)
    qn_ref[...] = lax.dot_general(msel_ref[...], q_sq, (((1,), (1,)), ((), ())), preferred_element_type=F32)
    k_sq = jnp.concatenate([ak * ak, bk * bk], axis=1).astype(BF16)
    g = gsum_ref[...]
    k_n2 = jnp.concatenate(lane_groups(k_sq, lambda t: jnp.dot(t, g, preferred_element_type=F32)), axis=1)
    k_max = jnp.broadcast_to(jnp.max(k_n2, axis=0, keepdims=True), kn_ref.shape)

    @pl.when(pl.program_id(1) == 0)
    def _():
        kn_ref[...] = k_max

    @pl.when(pl.program_id(1) != 0)
    def _():
        kn_ref[...] = jnp.maximum(kn_ref[...], k_max)

    v = jnp.dot(h, w_ref[:, 1664:2304], preferred_element_type=F32)
    vt = v.T
    ones = jnp.ones((ONES_ROWS, TILE), F32)
    pieces, col = [], 0
    for _, n in VT_BLOCKS:
        pieces += [vt[col:col + n], ones]
        col += n
    vt_ref[...] = jnp.concatenate(pieces, axis=0).astype(BF16)


def _even_projection(xs, mod, l, w_in, e, cos_t, sup_t, sdn_t, qg, kg, gsum, msel, n_batch):
    x_specs = _stream_specs(*xs)
    bsz, d = xs[0].shape[0], xs[0].shape[-1]
    t_all = xs[0].shape[1] + xs[1].shape[1]
    nt = t_all // TILE
    return pl.pallas_call(
        _proj_kernel,
        grid=(bsz, nt),
        in_specs=x_specs + [
                  pl.BlockSpec((None, None, N_MOD, d), _mod_row_map(l, n_batch, "both")),
                  _layer_spec(w_in, e),
                  pl.BlockSpec((TILE, LANES), lambda b, i: (i, 0)),
                  pl.BlockSpec((TILE, LANES), lambda b, i: (i, 0)),
                  pl.BlockSpec((TILE, LANES), lambda b, i: (i, 0)),
                  _const_spec(qg.shape), _const_spec(kg.shape), _const_spec(gsum.shape),
                  _const_spec(msel.shape)],
        out_specs=[pl.BlockSpec((None, TILE, Q_COLS), lambda b, i: (b, i, 0)),
                   pl.BlockSpec((None, TILE, K_COLS), lambda b, i: (b, i, 0)),
                   pl.BlockSpec((None, VT_ROWS, TILE), lambda b, i: (b, 0, i)),
                   pl.BlockSpec((None, N_MAPS, TILE), lambda b, i: (b, 0, i)),
                   pl.BlockSpec((None, SUBLANES, K_COLS), lambda b, i: (b, 0, 0))],
        out_shape=[jax.ShapeDtypeStruct((bsz, t_all, Q_COLS), BF16),
                   jax.ShapeDtypeStruct((bsz, t_all, K_COLS), BF16),
                   jax.ShapeDtypeStruct((bsz, VT_ROWS, t_all), BF16),
                   jax.ShapeDtypeStruct((bsz, N_MAPS, t_all), F32),
                   jax.ShapeDtypeStruct((bsz, SUBLANES, K_COLS), F32)],
        compiler_params=_params("arbitrary", "arbitrary"),
    )(*xs, mod, w_in, cos_t, sup_t, sdn_t, qg, kg, gsum, msel)


def _key_chunks(n_keys):
    sizes = ([n_keys % KEY_CHUNK] if n_keys % KEY_CHUNK else []) + [KEY_CHUNK] * (n_keys // KEY_CHUNK)
    starts = np.cumsum([0] + sizes[:-1])
    return [(int(a), int(n)) for a, n in zip(starts, sizes)]


def _attn_kernel(q_ref, k_ref, vt_ref, qn_ref, kn_ref, lamv_ref, subg_ref, o_ref, ot_ref, p0_ref, p1_ref,
                 mx_ref, *, lam_init):
    p_refs = (p0_ref, p1_ref)
    chunks = _key_chunks(k_ref.shape[0])
    upper_rows = lax.broadcasted_iota(jnp.int32, (LANES, TILE), 0) >= HEAD_DIM
    zero = jnp.zeros((LANES, TILE), BF16)

    lv = lamv_ref[...]
    lam = (jnp.exp(jnp.sum(lv[0:1] * lv[1:2], axis=-1, keepdims=True))
           - jnp.exp(jnp.sum(lv[2:3] * lv[3:4], axis=-1, keepdims=True)) + lam_init)
    subg = subg_ref[...]

    q0 = A_HEADS * LANES
    pairs = [(slice(hd * LANES, (hd + 1) * LANES), slice(hd * LANES, (hd + 1) * LANES),
              VT_BLOCKS[hd], VT_BLOCKS[hd]) for hd in range(A_HEADS)]
    pairs += [(slice(q0 + j * LANES, q0 + (j + 1) * LANES), slice(q0, q0 + LANES),
               VT_BLOCKS[A_HEADS], VT_BLOCKS[A_HEADS + 1]) for j in range(B_HEADS // 2)]
    n_pairs = len(pairs)

    q_t = q_ref[...].astype(F32).T

    def masked_q(i):
        qt = q_t[pairs[i][0], :].astype(BF16)
        return jnp.concatenate([jnp.where(upper_rows, zero, qt), jnp.where(upper_rows, qt, zero)], axis=1)

    def estimate(i):
        k_lo = pairs[i][1].start
        halves = [jnp.sqrt(qn_ref[2 * i + half:2 * i + half + 1, :]
                           * kn_ref[0:1, k_lo + half * HEAD_DIM:k_lo + half * HEAD_DIM + 1])
                  for half in range(2)]
        return jnp.concatenate(halves, axis=1) * SHIFT_MARGIN

    def score_chunk(i, qcat, start, size, shift, m8):
        s = jnp.dot(k_ref[start:start + size, pairs[i][1]], qcat, preferred_element_type=F32)
        p_refs[i % 2][start:start + size, :] = jnp.exp2(s - shift).astype(BF16)
        return jnp.maximum(m8, jnp.max(s.reshape(size // SUBLANES, SUBLANES, 2 * TILE), axis=0))

    def value_chunk(i, start, size, accs):
        _, _, blk_lo, blk_hi = pairs[i]
        p_ref = p_refs[i % 2]
        if blk_lo == blk_hi:
            base, n = blk_lo
            both = jnp.dot(vt_ref[base:base + n + ONES_ROWS, start:start + size], p_ref[start:start + size, :],
                           preferred_element_type=F32)
            parts = [both[:, :TILE], both[:, TILE:]]
        else:
            parts = [jnp.dot(vt_ref[base:base + n + ONES_ROWS, start:start + size],
                             p_ref[start:start + size, half * TILE:(half + 1) * TILE],
                             preferred_element_type=F32)
                     for half, (base, n) in enumerate((blk_lo, blk_hi))]
        return parts if accs is None else [a + b for a, b in zip(accs, parts)]

    def finish(i, accs):
        dens = [acc[n:n + 1] for (_, n), acc in zip(pairs[i][2:], accs)]
        o_lo, o_hi = [acc[:n] * (1.0 / den) for (_, n), acc, den in zip(pairs[i][2:], accs, dens)]
        if i < A_HEADS:
            o = o_lo - lam * o_hi
            ms = jnp.mean(o * o, axis=0, keepdims=True)
            ot_ref[i * A_VDIM:(i + 1) * A_VDIM, :] = o * lax.rsqrt(ms + EPS) * subg * (1.0 - lam_init)
        else:
            base = q0 + (i - A_HEADS) * LANES
            ot_ref[base:base + HEAD_DIM, :] = o_lo
            ot_ref[base + HEAD_DIM:base + LANES, :] = o_hi
        return dens

    mx_ref[...] = jnp.zeros_like(mx_ref)

    def attempt(carry):
        n_done, _ = carry
        den_min, den_max = None, None
        for t in range(n_pairs + 1):
            if t < n_pairs:
                qcat = masked_q(t)
                exact = jnp.max(mx_ref[t], axis=0, keepdims=True)
                shift = jnp.where(n_done == 0, estimate(t), exact)
                m8 = jnp.full((SUBLANES, 2 * TILE), NEG_BIG, F32)
            accs = None
            for start, size in chunks:
                if t < n_pairs:
                    m8 = score_chunk(t, qcat, start, size, shift, m8)
                if 1 <= t:
                    accs = value_chunk(t - 1, start, size, accs)
            if t < n_pairs:
                mx_ref[t] = m8
            if 1 <= t:
                for den in finish(t - 1, accs):
                    den_min = den if den_min is None else jnp.minimum(den_min, den)
                    den_max = den if den_max is None else jnp.maximum(den_max, den)
        safe = jnp.logical_and(jnp.min(den_min) >= DEN_SAFE_MIN, jnp.max(den_max) <= DEN_SAFE_MAX)
        return n_done + 1, safe

    def again(carry):
        n_done, safe = carry
        return jnp.logical_or(n_done == 0, jnp.logical_and(n_done == 1, jnp.logical_not(safe)))

    lax.while_loop(again, attempt, (jnp.int32(0), jnp.bool_(False)))
    o_ref[...] = ot_ref[...].T.astype(BF16)


def _attention(q_all, k_all, vt_all, qn_all, kn_all, lam_vecs, subg, lam_init, context):
    bsz, t_all, _ = q_all.shape
    n_keys = CTX_LEN if context else t_all
    nq = 1 if context else (t_all - CTX_LEN) // TILE
    q_off = 0 if context else CTX_LEN // TILE
    kern = functools.partial(_attn_kernel, lam_init=lam_init)
    per_batch = dict(pipeline_mode=pl.Buffered(1))
    return pl.pallas_call(
        kern,
        grid=(bsz, nq),
        in_specs=[pl.BlockSpec((None, TILE, Q_COLS), lambda b, i: (b, i + q_off, 0)),
                  pl.BlockSpec((None, n_keys, K_COLS), lambda b, i: (b, 0, 0), **per_batch),
                  pl.BlockSpec((None, VT_ROWS, n_keys), lambda b, i: (b, 0, 0), **per_batch),
                  pl.BlockSpec((None, N_MAPS, TILE), lambda b, i: (b, 0, i + q_off)),
                  pl.BlockSpec((None, SUBLANES, K_COLS), lambda b, i: (b, 0, 0)),
                  _const_spec(lam_vecs.shape), _const_spec(subg.shape)],
        out_specs=pl.BlockSpec((None, TILE, Q_COLS), lambda b, i: (b, i, 0)),
        out_shape=jax.ShapeDtypeStruct((bsz, nq * TILE, Q_COLS), BF16),
        scratch_shapes=[pltpu.VMEM((Q_COLS, TILE), F32),
                        pltpu.VMEM((n_keys, 2 * TILE), BF16),
                        pltpu.VMEM((n_keys, 2 * TILE), BF16),
                        pltpu.VMEM((N_MAPS // 2, SUBLANES, 2 * TILE), F32)],
        compiler_params=_params("arbitrary", "arbitrary"),
    )(q_all, k_all, vt_all, qn_all, kn_all, lam_vecs, subg)


def _post_kernel(o_ref, x_ref, mod_ref, ln_ref, wo_ref, w1_ref, w2_ref, out_ref, *, alpha):
    n = o_ref.shape[0] // TILE
    rows = [slice(k * TILE, (k + 1) * TILE) for k in range(n)]

    def write(k, v):
        out_ref[rows[k], :] = v

    y_fns = [lambda r=r: jnp.dot(o_ref[r, :], wo_ref[...], preferred_element_type=F32) for r in rows]
    _post_tiles([x_ref[r, :] for r in rows], y_fns, mod_ref, ln_ref, w1_ref, w2_ref, alpha, write)


def _even_post(o, x, mod, ln, l, w_out, e, w1, w2, n_batch, kind, alpha):
    bsz, t, d = x.shape
    rows = TILE * _tiles_per_step(t // TILE)
    blk = pl.BlockSpec((None, rows, d), lambda b, i: (b, i, 0))
    return pl.pallas_call(
        functools.partial(_post_kernel, alpha=alpha),
        grid=(bsz, t // rows),
        in_specs=[blk, blk,
                  pl.BlockSpec((None, None, N_MOD, d), _mod_row_map(l, n_batch, kind)),
                  _layer_spec(ln, l), _layer_spec(w_out, e), _layer_spec(w1, l), _layer_spec(w2, l)],
        out_specs=blk,
        out_shape=jax.ShapeDtypeStruct((bsz, t, d), F32),
        compiler_params=_params("arbitrary", "arbitrary"),
    )(o, x, mod, ln, w_out, w1, w2)


def _odd_kernel(x_ref, xp_ref, xn_ref, mod_ref, ln_ref, wi_ref, cw_ref, cb_ref, wo_ref, w1_ref, w2_ref,
                out_ref, u_ref, *, alpha):
    n_rows = x_ref.shape[0]
    rows = [slice(k * TILE, (k + 1) * TILE) for k in range(n_rows // TILE)]
    i = pl.program_id(1)
    first = i == 0
    last = i == pl.num_programs(1) - 1
    x = x_ref[...]
    shift, scale1 = mod_ref[0:1, :], 1.0 + mod_ref[1:2, :]
    x_ext = jnp.concatenate([xp_ref[...], x, xn_ref[...]], axis=0)
    h_ext = (x_ext * scale1 + shift).astype(BF16)
    z_c = jnp.dot(h_ext, wi_ref[:, D_MODEL:2 * D_MODEL], preferred_element_type=F32)
    z_x = jnp.dot(h_ext, wi_ref[:, 2 * D_MODEL:3 * D_MODEL], preferred_element_type=F32)
    row = lax.broadcasted_iota(jnp.int32, (n_rows + 2 * HALO, 1), 0)
    live_from = jnp.where(first, HALO, 0)
    live_to = jnp.where(last, HALO + n_rows, n_rows + 2 * HALO)
    live = jnp.logical_and(row >= live_from, row < live_to)
    u_ref[...] = jnp.where(live, z_c * z_x, 0.0)

    def mixer(r):
        lo = HALO + r.start
        conv = (u_ref[lo - 1:lo - 1 + TILE, :] * cw_ref[0:1, :] + u_ref[lo:lo + TILE, :] * cw_ref[1:2, :]
                + u_ref[lo + 1:lo + 1 + TILE, :] * cw_ref[2:3, :] + cb_ref[...])
        h = (x_ref[r, :] * scale1 + shift).astype(BF16)
        z_b = jnp.dot(h, wi_ref[:, 0:D_MODEL], preferred_element_type=F32)
        return jnp.dot((z_b * conv).astype(BF16), wo_ref[...], preferred_element_type=F32)

    def write(k, v):
        out_ref[rows[k], :] = v

    _post_tiles([x_ref[r, :] for r in rows], [lambda r=r: mixer(r) for r in rows],
                mod_ref, ln_ref, w1_ref, w2_ref, alpha, write)


def _odd_layer(x, mod, ln, l, w_in, conv_w, conv_b, w_out, od, w1, w2, n_batch, kind, alpha):
    bsz, t, d = x.shape
    rows = TILE * _tiles_per_step(t // TILE)
    per_step = rows // HALO
    n_halo_blocks = t // HALO
    blk = pl.BlockSpec((None, rows, d), lambda b, i: (b, i, 0))
    return pl.pallas_call(
        functools.partial(_odd_kernel, alpha=alpha),
        grid=(bsz, t // rows),
        in_specs=[blk,
                  pl.BlockSpec((None, HALO, d), lambda b, i: (b, jnp.maximum(i * per_step - 1, 0), 0)),
                  pl.BlockSpec((None, HALO, d),
                               lambda b, i: (b, jnp.minimum((i + 1) * per_step, n_halo_blocks - 1), 0)),
                  pl.BlockSpec((None, None, N_MOD, d), _mod_row_map(l, n_batch, kind)),
                  _layer_spec(ln, l), _layer_spec(w_in, od), _layer_spec(conv_w, od),
                  _layer_spec(conv_b, od), _layer_spec(w_out, od),
                  _layer_spec(w1, l), _layer_spec(w2, l)],
        out_specs=blk,
        out_shape=jax.ShapeDtypeStruct((bsz, t, d), F32),
        scratch_shapes=[pltpu.VMEM((rows + 2 * HALO, d), F32)],
        compiler_params=_params("arbitrary", "arbitrary"),
    )(x, x, x, mod, ln, w_in, conv_w, conv_b, w_out, w1, w2)


def _rope_tables(n_lat):
    t = jnp.arange(n_lat)
    pos = jnp.stack([(t // GRID_W).astype(F32), (t % GRID_W).astype(F32)], axis=-1)
    inv_freq = ROPE_THETA ** (-jnp.arange(ROPE_FREQS, dtype=F32) / ROPE_FREQS)
    ang = pos[:, :, None] * inv_freq
    cos = jnp.repeat(jnp.cos(ang), 2, axis=1).reshape(n_lat, HEAD_DIM)
    sin = jnp.repeat(jnp.sin(ang), 2, axis=1).reshape(n_lat, HEAD_DIM)
    first_half = (np.arange(HEAD_DIM) % (2 * ROPE_FREQS)) < ROPE_FREQS
    s_up = jnp.where(first_half, -sin, 0.0)
    s_dn = jnp.where(first_half, 0.0, sin)
    reps = LANES // HEAD_DIM
    pad = lambda a, fill: jnp.concatenate(
        [jnp.full((CTX_LEN, LANES), fill, F32), jnp.tile(a, (1, reps))], axis=0)
    return pad(cos, 1.0), pad(s_up, 0.0), pad(s_dn, 0.0)


def _gqa_head_order():
    half = B_HEADS // 2
    return [h for j in range(half) for h in (j, j + half)]


def kernel(x, c, ctx, c_ctx, ada_w, ada_b, attn_w_in, attn_w_out, diff_lambda, diff_subln_g, q_norm_g, k_norm_g, conv_w_in, conv_w, conv_b, conv_w_out, mlp_w1, mlp_w2, ln_g, ln_b):
    bsz, n_lat, d = x.shape
    depth = ada_w.shape[0]
    assert d == D_MODEL and ctx.shape[1] == CTX_LEN == TILE and n_lat % TILE == 0
    alpha = (2 * depth) ** 0.25

    n_rows = -(-(bsz + 1) // SUBLANES) * SUBLANES
    c_all = jnp.concatenate([c, c_ctx[None, :], jnp.zeros((n_rows - bsz - 1, d), F32)], axis=0)
    mod = _ada_modulation(c_all, ada_w, ada_b).reshape(depth, n_rows, N_MOD, d)

    cos_t, sup_t, sdn_t = _rope_tables(n_lat)
    order = _gqa_head_order()
    bq_segs = [(1536 + HEAD_DIM * h, 1536 + HEAD_DIM * (h + 1)) for h in order]
    in_segs = [(0, 512)] + bq_segs + [(512, 1024), (2048, 2176), (1024, 1536), (2176, 2304)]
    out_segs = [(0, 512)] + [(512 + HEAD_DIM * h, 512 + HEAD_DIM * (h + 1)) for h in order]
    gsum = jnp.asarray(np.kron(np.eye(LANES // HEAD_DIM), np.ones((HEAD_DIM, HEAD_DIM))), BF16)
    msel = jnp.asarray(np.kron(np.eye(N_MAPS), np.ones((1, HEAD_DIM))), BF16)
    ln = jnp.stack([ln_g[:, 0], ln_b[:, 0], ln_g[:, 1], ln_b[:, 1]], axis=1)

    w1_all, w2_all = mlp_w1.astype(BF16), mlp_w2.astype(BF16)
    attn_in = jnp.concatenate([attn_w_in[:, :, a:b] for a, b in in_segs], axis=2).astype(BF16)
    attn_out = jnp.concatenate([attn_w_out[:, a:b, :] for a, b in out_segs], axis=1).astype(BF16)
    conv_in, conv_out = conv_w_in.astype(BF16), conv_w_out.astype(BF16)
    conv_bias = conv_b[:, None, :]
    tile2 = lambda g: jnp.tile(g, LANES // HEAD_DIM)[None, :]

    x_ctx, x_lat = ctx, x
    for l in range(depth):
        ctx_out = any(j % 2 == 0 for j in range(l + 1, depth))
        if l % 2 == 0:
            e = l // 2
            lam_init = 0.8 - 0.6 * math.exp(-0.3 * l)
            q_all, k_all, vt_all, qn_all, kn_all = _even_projection(
                (x_ctx, x_lat), mod, l, attn_in, e, cos_t, sup_t, sdn_t,
                tile2(q_norm_g[e]), tile2(k_norm_g[e]), gsum, msel, bsz)
            attend = functools.partial(_attention, q_all, k_all, vt_all, qn_all, kn_all, diff_lambda[e],
                                       diff_subln_g[e][:, None], lam_init)
            post = functools.partial(_even_post, mod=mod, ln=ln, l=l, w_out=attn_out, e=e, w1=w1_all,
                                     w2=w2_all, n_batch=bsz, alpha=alpha)
            x_lat = post(attend(False), x_lat, kind="latent")
            x_ctx = post(attend(True), x_ctx, kind="context") if ctx_out else None
        else:
            od = l // 2
            odd = functools.partial(_odd_layer, mod=mod, ln=ln, l=l, w_in=conv_in, conv_w=conv_w,
                                    conv_b=conv_bias, w_out=conv_out, od=od, w1=w1_all, w2=w2_all,
                                    n_batch=bsz, alpha=alpha)
            x_lat = odd(x_lat, kind="latent")
            x_ctx = odd(x_ctx, kind="context") if ctx_out else None
    return x_lat
```

```python
import functools
import math

import numpy as np
import jax
import jax.numpy as jnp
from jax import lax
from jax.experimental import pallas as pl
from jax.experimental.pallas import tpu as pltpu

F32 = jnp.float32
BF16 = jnp.bfloat16

D_MODEL = 1024
CTX_LEN = 256
GRID_W = 64
HEAD_DIM = 64
A_HEADS = 4
A_VDIM = 2 * HEAD_DIM
B_HEADS = 8
B_KV_HEADS = 2
Q_COLS = 1024
K_COLS = 640
V_COLS = 640
ONES_ROWS = 16
MLP_HIDDEN = 4 * D_MODEL
ROPE_THETA = 10000.0
ROPE_FREQS = HEAD_DIM // 4
N_MOD = 6
EPS = 1e-6

TILE = 256
LANES = 128
SUBLANES = 8
HALO = SUBLANES
HID_CHUNK = 1024
KEY_CHUNK = 512
N_MAPS = 2 * A_HEADS + B_HEADS
N_DIFF_MAPS = 2 * A_HEADS
DIFF_K_COLS = A_HEADS * 2 * HEAD_DIM
SHIFT_MARGIN = 1.03
DEN_SAFE_MIN = 2.0 ** -60
DEN_SAFE_MAX = 2.0 ** 60
NEG_BIG = -1e30
VMEM_LIMIT = 56 * 1024 * 1024


def _const_spec(shape):
    nd = len(shape)
    return pl.BlockSpec(shape, lambda *_: (0,) * nd, pipeline_mode=pl.Buffered(1))


def _layer_spec(stacked, l):
    nd = stacked.ndim
    return pl.BlockSpec((None,) + stacked.shape[1:], lambda *_: (l,) + (0,) * (nd - 1),
                        pipeline_mode=pl.Buffered(1))


def _stream_specs(x_ctx, x_lat):
    d = x_ctx.shape[-1]
    n_ctx = CTX_LEN // TILE
    return [pl.BlockSpec((None, TILE, d), lambda b, i: (b, 0, 0)),
            pl.BlockSpec((None, TILE, d), lambda b, i: (b, jnp.maximum(i - n_ctx, 0), 0))]


def _read_stream(ctx_ref, lat_ref):
    return jnp.where(pl.program_id(1) == 0, ctx_ref[...], lat_ref[...])


def _params(*sem):
    return pltpu.CompilerParams(dimension_semantics=sem, vmem_limit_bytes=VMEM_LIMIT)


def _ada_kernel(c_ref, w_ref, b_ref, o_ref):
    c = c_ref[...]
    act = c * (1.0 / (1.0 + jnp.exp(-c)))
    o_ref[...] = jnp.dot(act.astype(BF16), w_ref[...].astype(BF16),
                         preferred_element_type=F32) + b_ref[...]


def _ada_modulation(c_all, ada_w, ada_b):
    depth, d, n = ada_w.shape
    rows = c_all.shape[0]
    tn = 1536
    return pl.pallas_call(
        _ada_kernel,
        grid=(depth, n // tn),
        in_specs=[pl.BlockSpec((rows, d), lambda l, j: (0, 0)),
                  pl.BlockSpec((None, d, tn), lambda l, j: (l, 0, j)),
                  pl.BlockSpec((None, 1, tn), lambda l, j: (l, 0, j))],
        out_specs=pl.BlockSpec((None, rows, tn), lambda l, j: (l, 0, j)),
        out_shape=jax.ShapeDtypeStruct((depth, rows, n), F32),
        compiler_params=_params("arbitrary", "arbitrary"),
    )(c_all, ada_w, ada_b.reshape(depth, 1, n))


def _layer_norm(v, g, b):
    mu = jnp.mean(v, axis=-1, keepdims=True)
    d = v - mu
    var = jnp.mean(d * d, axis=-1, keepdims=True)
    return d * lax.rsqrt(var + EPS) * g + b


def _post_tiles(xs, y_fns, mod_ref, ln_ref, w1_ref, w2_ref, alpha, write):
    n_c = MLP_HIDDEN // HID_CHUNK

    def ln1(x, y):
        x1 = _layer_norm(alpha * x + mod_ref[2:3, :] * y, ln_ref[0:1, :], ln_ref[1:2, :])
        return x1, (x1 * (1.0 + mod_ref[4:5, :]) + mod_ref[3:4, :]).astype(BF16)

    def mlp(h2, chunks, y2):
        for c in chunks:
            sl = slice(c * HID_CHUNK, (c + 1) * HID_CHUNK)
            hid = jnp.maximum(jnp.dot(h2, w1_ref[:, sl], preferred_element_type=F32), 0.0)
            part = jnp.dot((hid * hid).astype(BF16), w2_ref[sl, :], preferred_element_type=F32)
            y2 = part if y2 is None else y2 + part
        return y2

    def ln2(x1, y2):
        return _layer_norm(alpha * x1 + mod_ref[5:6, :] * y2, ln_ref[2:3, :], ln_ref[3:4, :])

    x1a, h2a = ln1(xs[0], y_fns[0]())
    if len(xs) == 1:
        write(0, ln2(x1a, mlp(h2a, range(n_c), None)))
        return
    yb = y_fns[1]()
    y2a = mlp(h2a, range(0, n_c // 2), None)
    x1b, h2b = ln1(xs[1], yb)
    y2a = mlp(h2a, range(n_c // 2, n_c), y2a)
    y2b = mlp(h2b, range(0, n_c // 2), None)
    write(0, ln2(x1a, y2a))
    y2b = mlp(h2b, range(n_c // 2, n_c), y2b)
    write(1, ln2(x1b, y2b))


def _tiles_per_step(n_tiles):
    return 2 if n_tiles % 2 == 0 else 1


def _mod_row_map(l, n_batch, kind):
    if kind == "both":
        return lambda b, i: (l, jnp.where(i == 0, n_batch, b), 0, 0)
    if kind == "context":
        return lambda b, i: (l, n_batch, 0, 0)
    return lambda b, i: (l, b, 0, 0)


def _vt_row_layout():
    blocks, base = [], 0
    for n in [A_VDIM] * A_HEADS + [HEAD_DIM] * B_KV_HEADS:
        blocks.append((base, n))
        base += n + ONES_ROWS
    return blocks, base


VT_BLOCKS, VT_ROWS = _vt_row_layout()


def _rope(z, cos, s_up, s_dn):
    n = z.shape[-1]
    return z * cos + pltpu.roll(z, n - ROPE_FREQS, 1) * s_up + pltpu.roll(z, ROPE_FREQS, 1) * s_dn


def _group_mean_sq(z, gsum_ref):
    z2 = z * z
    hi = z2.astype(BF16)
    lo = (z2 - hi.astype(F32)).astype(BF16)
    g = gsum_ref[...]
    ss = jnp.dot(hi, g, preferred_element_type=F32) + jnp.dot(lo, g, preferred_element_type=F32)
    return ss * (1.0 / HEAD_DIM)


def _proj_kernel(xc_ref, xl_ref, mod_ref, w_ref, cos_ref, sup_ref, sdn_ref, qg_ref, kg_ref, gsum_ref, msel_ref,
                 q_ref, k_ref, vt_ref, qn_ref, kn_ref):
    h = (_read_stream(xc_ref, xl_ref) * (1.0 + mod_ref[1:2, :]) + mod_ref[0:1, :]).astype(BF16)
    cos, s_up, s_dn = cos_ref[...], sup_ref[...], sdn_ref[...]
    scale = HEAD_DIM ** -0.5 * math.log2(math.e)

    def lane_groups(z, fn):
        return [fn(z[:, g * LANES:(g + 1) * LANES]) for g in range(z.shape[1] // LANES)]

    def rope_all(z, mul):
        return jnp.concatenate(lane_groups(z, lambda t: _rope(t, cos, s_up, s_dn) * mul), axis=1)

    def norm_all(z, gain_ref):
        gain = gain_ref[...]
        return jnp.concatenate(
            lane_groups(z, lambda t: t * lax.rsqrt(_group_mean_sq(t, gsum_ref) + EPS) * gain), axis=1)

    aq = rope_all(jnp.dot(h, w_ref[:, 0:512], preferred_element_type=F32), scale)
    q_ref[:, 0:512] = aq.astype(BF16)
    bq = rope_all(norm_all(jnp.dot(h, w_ref[:, 512:1024], preferred_element_type=F32), qg_ref), scale)
    q_ref[:, 512:1024] = bq.astype(BF16)
    ak = rope_all(jnp.dot(h, w_ref[:, 1024:1536], preferred_element_type=F32), 1.0)
    k_ref[:, 0:512] = ak.astype(BF16)
    bk = rope_all(norm_all(jnp.dot(h, w_ref[:, 1536:1664], preferred_element_type=F32), kg_ref), 1.0)
    k_ref[:, 512:640] = bk.astype(BF16)

    q_sq = (aq * aq).astype(BF16)
    qn_ref[...] = lax.dot_general(msel_ref[...], q_sq, (((1,), (1,)), ((), ())), preferred_element_type=F32)
    k_sq = (ak * ak).astype(BF16)
    g = gsum_ref[...]
    k_n2 = jnp.concatenate(lane_groups(k_sq, lambda t: jnp.dot(t, g, preferred_element_type=F32)), axis=1)
    k_max = jnp.broadcast_to(jnp.max(k_n2, axis=0, keepdims=True), kn_ref.shape)

    v = jnp.dot(h, w_ref[:, 1664:2304], preferred_element_type=F32)
    vt = v.T
    ones = jnp.ones((ONES_ROWS, TILE), F32)
    pieces, col = [], 0
    for _, n in VT_BLOCKS:
        pieces += [vt[col:col + n], ones]
        col += n
    vt_ref[...] = jnp.concatenate(pieces, axis=0).astype(BF16)

    @pl.when(pl.program_id(1) == 0)
    def _():
        kn_ref[...] = k_max

    @pl.when(pl.program_id(1) != 0)
    def _():
        kn_ref[...] = jnp.maximum(kn_ref[...], k_max)


def _even_projection(xs, mod, l, w_in, e, cos_t, sup_t, sdn_t, qg, kg, gsum, msel, n_batch):
    x_specs = _stream_specs(*xs)
    bsz, d = xs[0].shape[0], xs[0].shape[-1]
    t_all = xs[0].shape[1] + xs[1].shape[1]
    nt = t_all // TILE
    return pl.pallas_call(
        _proj_kernel,
        grid=(bsz, nt),
        in_specs=x_specs + [
                  pl.BlockSpec((None, None, N_MOD, d), _mod_row_map(l, n_batch, "both")),
                  _layer_spec(w_in, e),
                  pl.BlockSpec((TILE, LANES), lambda b, i: (i, 0)),
                  pl.BlockSpec((TILE, LANES), lambda b, i: (i, 0)),
                  pl.BlockSpec((TILE, LANES), lambda b, i: (i, 0)),
                  _const_spec(qg.shape), _const_spec(kg.shape), _const_spec(gsum.shape),
                  _const_spec(msel.shape)],
        out_specs=[pl.BlockSpec((None, TILE, Q_COLS), lambda b, i: (b, i, 0)),
                   pl.BlockSpec((None, TILE, K_COLS), lambda b, i: (b, i, 0)),
                   pl.BlockSpec((None, VT_ROWS, TILE), lambda b, i: (b, 0, i)),
                   pl.BlockSpec((None, N_DIFF_MAPS, TILE), lambda b, i: (b, 0, i)),
                   pl.BlockSpec((None, SUBLANES, DIFF_K_COLS), lambda b, i: (b, 0, 0))],
        out_shape=[jax.ShapeDtypeStruct((bsz, t_all, Q_COLS), BF16),
                   jax.ShapeDtypeStruct((bsz, t_all, K_COLS), BF16),
                   jax.ShapeDtypeStruct((bsz, VT_ROWS, t_all), BF16),
                   jax.ShapeDtypeStruct((bsz, N_DIFF_MAPS, t_all), F32),
                   jax.ShapeDtypeStruct((bsz, SUBLANES, DIFF_K_COLS), F32)],
        compiler_params=_params("arbitrary", "arbitrary"),
    )(*xs, mod, w_in, cos_t, sup_t, sdn_t, qg, kg, gsum, msel)


def _key_chunks(n_keys):
    sizes = ([n_keys % KEY_CHUNK] if n_keys % KEY_CHUNK else []) + [KEY_CHUNK] * (n_keys // KEY_CHUNK)
    starts = np.cumsum([0] + sizes[:-1])
    return [(int(a), int(n)) for a, n in zip(starts, sizes)]


def _attn_kernel(q_ref, k_ref, vt_ref, qn_ref, kn_ref, qg_ref, kg_ref, lamv_ref, subg_ref, o_ref, ot_ref, p0_ref, p1_ref,
                 mx_ref, *, lam_init):
    p_refs = (p0_ref, p1_ref)
    chunks = _key_chunks(k_ref.shape[0])
    upper_rows = lax.broadcasted_iota(jnp.int32, (LANES, TILE), 0) >= HEAD_DIM
    zero = jnp.zeros((LANES, TILE), BF16)

    lv = lamv_ref[...]
    lam = (jnp.exp(jnp.sum(lv[0:1] * lv[1:2], axis=-1, keepdims=True))
           - jnp.exp(jnp.sum(lv[2:3] * lv[3:4], axis=-1, keepdims=True)) + lam_init)
    subg = subg_ref[...]

    q0 = A_HEADS * LANES
    pairs = [(slice(hd * LANES, (hd + 1) * LANES), slice(hd * LANES, (hd + 1) * LANES),
              VT_BLOCKS[hd], VT_BLOCKS[hd]) for hd in range(A_HEADS)]
    pairs += [(slice(q0 + j * LANES, q0 + (j + 1) * LANES), slice(q0, q0 + LANES),
               VT_BLOCKS[A_HEADS], VT_BLOCKS[A_HEADS + 1]) for j in range(B_HEADS // 2)]
    n_pairs = len(pairs)

    q_t = q_ref[...].astype(F32).T

    def masked_q(i):
        qt = q_t[pairs[i][0], :].astype(BF16)
        return jnp.concatenate([jnp.where(upper_rows, zero, qt), jnp.where(upper_rows, qt, zero)], axis=1)

    def estimate(i):
        if i >= A_HEADS:
            bound = (jnp.max(jnp.abs(qg_ref[...]), axis=-1, keepdims=True)
                     * jnp.max(jnp.abs(kg_ref[...]), axis=-1, keepdims=True)
                     * (HEAD_DIM ** 0.5 * math.log2(math.e) * SHIFT_MARGIN))
            return jnp.broadcast_to(bound, (1, 2 * TILE))
        k_lo = pairs[i][1].start
        halves = [jnp.sqrt(qn_ref[2 * i + half:2 * i + half + 1, :]
                           * kn_ref[0:1, k_lo + half * HEAD_DIM:k_lo + half * HEAD_DIM + 1])
                  for half in range(2)]
        return jnp.concatenate(halves, axis=1) * SHIFT_MARGIN

    def score_chunk(i, qcat, start, size, shift, m8):
        s = jnp.dot(k_ref[start:start + size, pairs[i][1]], qcat, preferred_element_type=F32)
        p_refs[i % 2][start:start + size, :] = jnp.exp2(s - shift).astype(BF16)
        return jnp.maximum(m8, jnp.max(s.reshape(size // SUBLANES, SUBLANES, 2 * TILE), axis=0))

    def value_chunk(i, start, size, accs):
        _, _, blk_lo, blk_hi = pairs[i]
        p_ref = p_refs[i % 2]
        if blk_lo == blk_hi:
            base, n = blk_lo
            both = jnp.dot(vt_ref[base:base + n + ONES_ROWS, start:start + size], p_ref[start:start + size, :],
                           preferred_element_type=F32)
            parts = [both[:, :TILE], both[:, TILE:]]
        else:
            parts = [jnp.dot(vt_ref[base:base + n + ONES_ROWS, start:start + size],
                             p_ref[start:start + size, half * TILE:(half + 1) * TILE],
                             preferred_element_type=F32)
                     for half, (base, n) in enumerate((blk_lo, blk_hi))]
        return parts if accs is None else [a + b for a, b in zip(accs, parts)]

    def finish(i, accs):
        dens = [acc[n:n + 1] for (_, n), acc in zip(pairs[i][2:], accs)]
        o_lo, o_hi = [acc[:n] * (1.0 / den) for (_, n), acc, den in zip(pairs[i][2:], accs, dens)]
        if i < A_HEADS:
            o = o_lo - lam * o_hi
            ms = jnp.mean(o * o, axis=0, keepdims=True)
            ot_ref[i * A_VDIM:(i + 1) * A_VDIM, :] = o * lax.rsqrt(ms + EPS) * subg * (1.0 - lam_init)
        else:
            base = q0 + (i - A_HEADS) * LANES
            ot_ref[base:base + HEAD_DIM, :] = o_lo
            ot_ref[base + HEAD_DIM:base + LANES, :] = o_hi
        return dens

    mx_ref[...] = jnp.zeros_like(mx_ref)

    def attempt(carry):
        n_done, _ = carry
        den_min, den_max = None, None
        for t in range(n_pairs + 1):
            if t < n_pairs:
                qcat = masked_q(t)
                exact = jnp.max(mx_ref[t], axis=0, keepdims=True)
                shift = jnp.where(n_done == 0, estimate(t), exact)
                m8 = jnp.full((SUBLANES, 2 * TILE), NEG_BIG, F32)
            accs = None
            for start, size in chunks:
                if t < n_pairs:
                    m8 = score_chunk(t, qcat, start, size, shift, m8)
                if 1 <= t:
                    accs = value_chunk(t - 1, start, size, accs)
            if t < n_pairs:
                mx_ref[t] = m8
            if 1 <= t:
                for den in finish(t - 1, accs):
                    den_min = den if den_min is None else jnp.minimum(den_min, den)
                    den_max = den if den_max is None else jnp.maximum(den_max, den)
        safe = jnp.logical_and(jnp.min(den_min) >= DEN_SAFE_MIN, jnp.max(den_max) <= DEN_SAFE_MAX)
        return n_done + 1, safe

    def again(carry):
        n_done, safe = carry
        return jnp.logical_or(n_done == 0, jnp.logical_and(n_done == 1, jnp.logical_not(safe)))

    lax.while_loop(again, attempt, (jnp.int32(0), jnp.bool_(False)))
    o_ref[...] = ot_ref[...].T.astype(BF16)


def _attention(q_all, k_all, vt_all, qn_all, kn_all, qg, kg, lam_vecs, subg, lam_init, context):
    bsz, t_all, _ = q_all.shape
    n_keys = CTX_LEN if context else t_all
    nq = 1 if context else (t_all - CTX_LEN) // TILE
    q_off = 0 if context else CTX_LEN // TILE
    kern = functools.partial(_attn_kernel, lam_init=lam_init)
    per_batch = dict(pipeline_mode=pl.Buffered(1))
    return pl.pallas_call(
        kern,
        grid=(bsz, nq),
        in_specs=[pl.BlockSpec((None, TILE, Q_COLS), lambda b, i: (b, i + q_off, 0)),
                  pl.BlockSpec((None, n_keys, K_COLS), lambda b, i: (b, 0, 0), **per_batch),
                  pl.BlockSpec((None, VT_ROWS, n_keys), lambda b, i: (b, 0, 0), **per_batch),
                  pl.BlockSpec((None, N_DIFF_MAPS, TILE), lambda b, i: (b, 0, i + q_off)),
                  pl.BlockSpec((None, SUBLANES, DIFF_K_COLS), lambda b, i: (b, 0, 0)),
                  _const_spec(qg.shape), _const_spec(kg.shape),
                  _const_spec(lam_vecs.shape), _const_spec(subg.shape)],
        out_specs=pl.BlockSpec((None, TILE, Q_COLS), lambda b, i: (b, i, 0)),
        out_shape=jax.ShapeDtypeStruct((bsz, nq * TILE, Q_COLS), BF16),
        scratch_shapes=[pltpu.VMEM((Q_COLS, TILE), F32),
                        pltpu.VMEM((n_keys, 2 * TILE), BF16),
                        pltpu.VMEM((n_keys, 2 * TILE), BF16),
                        pltpu.VMEM((N_MAPS // 2, SUBLANES, 2 * TILE), F32)],
        compiler_params=_params("arbitrary", "arbitrary"),
    )(q_all, k_all, vt_all, qn_all, kn_all, qg, kg, lam_vecs, subg)


def _post_kernel(o_ref, x_ref, mod_ref, ln_ref, wo_ref, w1_ref, w2_ref, out_ref, *, alpha):
    n = o_ref.shape[0] // TILE
    rows = [slice(k * TILE, (k + 1) * TILE) for k in range(n)]

    def write(k, v):
        out_ref[rows[k], :] = v

    y_fns = [lambda r=r: jnp.dot(o_ref[r, :], wo_ref[...], preferred_element_type=F32) for r in rows]
    _post_tiles([x_ref[r, :] for r in rows], y_fns, mod_ref, ln_ref, w1_ref, w2_ref, alpha, write)


def _even_post(o, x, mod, ln, l, w_out, e, w1, w2, n_batch, kind, alpha):
    bsz, t, d = x.shape
    rows = TILE * _tiles_per_step(t // TILE)
    blk = pl.BlockSpec((None, rows, d), lambda b, i: (b, i, 0))
    return pl.pallas_call(
        functools.partial(_post_kernel, alpha=alpha),
        grid=(bsz, t // rows),
        in_specs=[blk, blk,
                  pl.BlockSpec((None, None, N_MOD, d), _mod_row_map(l, n_batch, kind)),
                  _layer_spec(ln, l), _layer_spec(w_out, e), _layer_spec(w1, l), _layer_spec(w2, l)],
        out_specs=blk,
        out_shape=jax.ShapeDtypeStruct((bsz, t, d), F32),
        compiler_params=_params("arbitrary", "arbitrary"),
    )(o, x, mod, ln, w_out, w1, w2)


def _odd_kernel(x_ref, xp_ref, xn_ref, mod_ref, ln_ref, wi_ref, cw_ref, cb_ref, wo_ref, w1_ref, w2_ref,
                out_ref, u_ref, *, alpha):
    n_rows = x_ref.shape[0]
    rows = [slice(k * TILE, (k + 1) * TILE) for k in range(n_rows // TILE)]
    i = pl.program_id(1)
    first = i == 0
    last = i == pl.num_programs(1) - 1
    x = x_ref[...]
    shift, scale1 = mod_ref[0:1, :], 1.0 + mod_ref[1:2, :]
    x_ext = jnp.concatenate([xp_ref[...], x, xn_ref[...]], axis=0)
    h_ext = (x_ext * scale1 + shift).astype(BF16)
    z_c = jnp.dot(h_ext, wi_ref[:, D_MODEL:2 * D_MODEL], preferred_element_type=F32)
    z_x = jnp.dot(h_ext, wi_ref[:, 2 * D_MODEL:3 * D_MODEL], preferred_element_type=F32)
    row = lax.broadcasted_iota(jnp.int32, (n_rows + 2 * HALO, 1), 0)
    live_from = jnp.where(first, HALO, 0)
    live_to = jnp.where(last, HALO + n_rows, n_rows + 2 * HALO)
    live = jnp.logical_and(row >= live_from, row < live_to)
    u_ref[...] = jnp.where(live, z_c * z_x, 0.0)

    def mixer(r):
        lo = HALO + r.start
        conv = (u_ref[lo - 1:lo - 1 + TILE, :] * cw_ref[0:1, :] + u_ref[lo:lo + TILE, :] * cw_ref[1:2, :]
                + u_ref[lo + 1:lo + 1 + TILE, :] * cw_ref[2:3, :] + cb_ref[...])
        h = (x_ref[r, :] * scale1 + shift).astype(BF16)
        z_b = jnp.dot(h, wi_ref[:, 0:D_MODEL], preferred_element_type=F32)
        return jnp.dot((z_b * conv).astype(BF16), wo_ref[...], preferred_element_type=F32)

    def write(k, v):
        out_ref[rows[k], :] = v

    _post_tiles([x_ref[r, :] for r in rows], [lambda r=r: mixer(r) for r in rows],
                mod_ref, ln_ref, w1_ref, w2_ref, alpha, write)


def _odd_layer(x, mod, ln, l, w_in, conv_w, conv_b, w_out, od, w1, w2, n_batch, kind, alpha):
    bsz, t, d = x.shape
    rows = TILE * _tiles_per_step(t // TILE)
    per_step = rows // HALO
    n_halo_blocks = t // HALO
    blk = pl.BlockSpec((None, rows, d), lambda b, i: (b, i, 0))
    return pl.pallas_call(
        functools.partial(_odd_kernel, alpha=alpha),
        grid=(bsz, t // rows),
        in_specs=[blk,
                  pl.BlockSpec((None, HALO, d), lambda b, i: (b, jnp.maximum(i * per_step - 1, 0), 0)),
                  pl.BlockSpec((None, HALO, d),
                               lambda b, i: (b, jnp.minimum((i + 1) * per_step, n_halo_blocks - 1), 0)),
                  pl.BlockSpec((None, None, N_MOD, d), _mod_row_map(l, n_batch, kind)),
                  _layer_spec(ln, l), _layer_spec(w_in, od), _layer_spec(conv_w, od),
                  _layer_spec(conv_b, od), _layer_spec(w_out, od),
                  _layer_spec(w1, l), _layer_spec(w2, l)],
        out_specs=blk,
        out_shape=jax.ShapeDtypeStruct((bsz, t, d), F32),
        scratch_shapes=[pltpu.VMEM((rows + 2 * HALO, d), F32)],
        compiler_params=_params("arbitrary", "arbitrary"),
    )(x, x, x, mod, ln, w_in, conv_w, conv_b, w_out, w1, w2)


def _rope_tables(n_lat):
    t = jnp.arange(n_lat)
    pos = jnp.stack([(t // GRID_W).astype(F32), (t % GRID_W).astype(F32)], axis=-1)
    inv_freq = ROPE_THETA ** (-jnp.arange(ROPE_FREQS, dtype=F32) / ROPE_FREQS)
    ang = pos[:, :, None] * inv_freq
    cos = jnp.repeat(jnp.cos(ang), 2, axis=1).reshape(n_lat, HEAD_DIM)
    sin = jnp.repeat(jnp.sin(ang), 2, axis=1).reshape(n_lat, HEAD_DIM)
    first_half = (np.arange(HEAD_DIM) % (2 * ROPE_FREQS)) < ROPE_FREQS
    s_up = jnp.where(first_half, -sin, 0.0)
    s_dn = jnp.where(first_half, 0.0, sin)
    reps = LANES // HEAD_DIM
    pad = lambda a, fill: jnp.concatenate(
        [jnp.full((CTX_LEN, LANES), fill, F32), jnp.tile(a, (1, reps))], axis=0)
    return pad(cos, 1.0), pad(s_up, 0.0), pad(s_dn, 0.0)


def _gqa_head_order():
    half = B_HEADS // 2
    return [h for j in range(half) for h in (j, j + half)]


def kernel(x, c, ctx, c_ctx, ada_w, ada_b, attn_w_in, attn_w_out, diff_lambda, diff_subln_g, q_norm_g, k_norm_g, conv_w_in, conv_w, conv_b, conv_w_out, mlp_w1, mlp_w2, ln_g, ln_b):
    bsz, n_lat, d = x.shape
    depth = ada_w.shape[0]
    assert d == D_MODEL and ctx.shape[1] == CTX_LEN == TILE and n_lat % TILE == 0
    alpha = (2 * depth) ** 0.25

    n_rows = -(-(bsz + 1) // SUBLANES) * SUBLANES
    c_all = jnp.concatenate([c, c_ctx[None, :], jnp.zeros((n_rows - bsz - 1, d), F32)], axis=0)
    mod = _ada_modulation(c_all, ada_w, ada_b).reshape(depth, n_rows, N_MOD, d)

    cos_t, sup_t, sdn_t = _rope_tables(n_lat)
    order = _gqa_head_order()
    bq_segs = [(1536 + HEAD_DIM * h, 1536 + HEAD_DIM * (h + 1)) for h in order]
    in_segs = [(0, 512)] + bq_segs + [(512, 1024), (2048, 2176), (1024, 1536), (2176, 2304)]
    out_segs = [(0, 512)] + [(512 + HEAD_DIM * h, 512 + HEAD_DIM * (h + 1)) for h in order]
    gsum = jnp.asarray(np.kron(np.eye(LANES // HEAD_DIM), np.ones((HEAD_DIM, HEAD_DIM))), BF16)
    msel = jnp.asarray(np.kron(np.eye(N_DIFF_MAPS), np.ones((1, HEAD_DIM))), BF16)
    ln = jnp.stack([ln_g[:, 0], ln_b[:, 0], ln_g[:, 1], ln_b[:, 1]], axis=1)

    w1_all, w2_all = mlp_w1.astype(BF16), mlp_w2.astype(BF16)
    attn_in = jnp.concatenate([attn_w_in[:, :, a:b] for a, b in in_segs], axis=2).astype(BF16)
    attn_out = jnp.concatenate([attn_w_out[:, a:b, :] for a, b in out_segs], axis=1).astype(BF16)
    conv_in, conv_out = conv_w_in.astype(BF16), conv_w_out.astype(BF16)
    conv_bias = conv_b[:, None, :]
    tile2 = lambda g: jnp.tile(g, LANES // HEAD_DIM)[None, :]

    x_ctx, x_lat = ctx, x
    for l in range(depth):
        ctx_out = any(j % 2 == 0 for j in range(l + 1, depth))
        if l % 2 == 0:
            e = l // 2
            lam_init = 0.8 - 0.6 * math.exp(-0.3 * l)
            qg, kg = tile2(q_norm_g[e]), tile2(k_norm_g[e])
            q_all, k_all, vt_all, qn_all, kn_all = _even_projection(
                (x_ctx, x_lat), mod, l, attn_in, e, cos_t, sup_t, sdn_t, qg, kg, gsum, msel, bsz)
            attend = functools.partial(_attention, q_all, k_all, vt_all, qn_all, kn_all, qg, kg, diff_lambda[e],
                                       diff_subln_g[e][:, None], lam_init)
            post = functools.partial(_even_post, mod=mod, ln=ln, l=l, w_out=attn_out, e=e, w1=w1_all,
                                     w2=w2_all, n_batch=bsz, alpha=alpha)
            x_lat = post(attend(False), x_lat, kind="latent")
            x_ctx = post(attend(True), x_ctx, kind="context") if ctx_out else None
        else:
            od = l // 2
            odd = functools.partial(_odd_layer, mod=mod, ln=ln, l=l, w_in=conv_in, conv_w=conv_w,
                                    conv_b=conv_bias, w_out=conv_out, od=od, w1=w1_all, w2=w2_all,
                                    n_batch=bsz, alpha=alpha)
            x_lat = odd(x_lat, kind="latent")
            x_ctx = odd(x_ctx, kind="context") if ctx_out else None
    return x_lat
```

```python
import functools
import math

import numpy as np
import jax
import jax.numpy as jnp
from jax import lax
from jax.experimental import pallas as pl
from jax.experimental.pallas import tpu as pltpu

F32 = jnp.float32
BF16 = jnp.bfloat16

D_MODEL = 1024
CTX_LEN = 256
GRID_W = 64
HEAD_DIM = 64
A_HEADS = 4
A_VDIM = 2 * HEAD_DIM
B_HEADS = 8
B_KV_HEADS = 2
Q_COLS = 1024
K_COLS = 640
V_COLS = 640
ONES_ROWS = 16
MLP_HIDDEN = 4 * D_MODEL
ROPE_THETA = 10000.0
ROPE_FREQS = HEAD_DIM // 4
N_MOD = 6
EPS = 1e-6

TILE = 256
LANES = 128
SUBLANES = 8
HALO = SUBLANES
HID_CHUNK = 1024
KEY_CHUNK = 512
N_MAPS = 2 * A_HEADS + B_HEADS
N_DIFF_MAPS = 2 * A_HEADS
DIFF_K_COLS = A_HEADS * 2 * HEAD_DIM
SHIFT_MARGIN = 1.03
DEN_SAFE_MIN = 2.0 ** -60
DEN_SAFE_MAX = 2.0 ** 60
NEG_BIG = -1e30
VMEM_LIMIT = 56 * 1024 * 1024


def _const_spec(shape):
    nd = len(shape)
    return pl.BlockSpec(shape, lambda *_: (0,) * nd, pipeline_mode=pl.Buffered(1))


def _layer_spec(stacked, l):
    nd = stacked.ndim
    return pl.BlockSpec((None,) + stacked.shape[1:], lambda *_: (l,) + (0,) * (nd - 1),
                        pipeline_mode=pl.Buffered(1))


def _stream_specs(x_ctx, x_lat):
    d = x_ctx.shape[-1]
    n_ctx = CTX_LEN // TILE
    return [pl.BlockSpec((None, TILE, d), lambda b, i: (b, 0, 0)),
            pl.BlockSpec((None, TILE, d), lambda b, i: (b, jnp.maximum(i - n_ctx, 0), 0))]


def _read_stream(ctx_ref, lat_ref):
    return jnp.where(pl.program_id(1) == 0, ctx_ref[...], lat_ref[...])


def _params(*sem):
    return pltpu.CompilerParams(dimension_semantics=sem, vmem_limit_bytes=VMEM_LIMIT)


def _ada_kernel(c_ref, w_ref, b_ref, o_ref):
    c = c_ref[...]
    act = c * (1.0 / (1.0 + jnp.exp(-c)))
    o_ref[...] = jnp.dot(act.astype(BF16), w_ref[...].astype(BF16),
                         preferred_element_type=F32) + b_ref[...]


def _ada_modulation(c_all, ada_w, ada_b):
    depth, d, n = ada_w.shape
    rows = c_all.shape[0]
    tn = 1536
    return pl.pallas_call(
        _ada_kernel,
        grid=(depth, n // tn),
        in_specs=[pl.BlockSpec((rows, d), lambda l, j: (0, 0)),
                  pl.BlockSpec((None, d, tn), lambda l, j: (l, 0, j)),
                  pl.BlockSpec((None, 1, tn), lambda l, j: (l, 0, j))],
        out_specs=pl.BlockSpec((None, rows, tn), lambda l, j: (l, 0, j)),
        out_shape=jax.ShapeDtypeStruct((depth, rows, n), F32),
        compiler_params=_params("arbitrary", "arbitrary"),
    )(c_all, ada_w, ada_b.reshape(depth, 1, n))


def _layer_norm(v, g, b):
    mu = jnp.mean(v, axis=-1, keepdims=True)
    d = v - mu
    var = jnp.mean(d * d, axis=-1, keepdims=True)
    return d * lax.rsqrt(var + EPS) * g + b


def _post_tiles(xs, y_fns, mod_ref, ln_ref, w1_ref, w2_ref, alpha, write):
    n_c = MLP_HIDDEN // HID_CHUNK

    def ln1(x, y):
        x1 = _layer_norm(alpha * x + mod_ref[2:3, :] * y, ln_ref[0:1, :], ln_ref[1:2, :])
        return x1, (x1 * (1.0 + mod_ref[4:5, :]) + mod_ref[3:4, :]).astype(BF16)

    def mlp(h2, chunks, y2):
        for c in chunks:
            sl = slice(c * HID_CHUNK, (c + 1) * HID_CHUNK)
            hid = jnp.maximum(jnp.dot(h2, w1_ref[:, sl], preferred_element_type=F32), 0.0)
            part = jnp.dot((hid * hid).astype(BF16), w2_ref[sl, :], preferred_element_type=F32)
            y2 = part if y2 is None else y2 + part
        return y2

    def ln2(x1, y2):
        return _layer_norm(alpha * x1 + mod_ref[5:6, :] * y2, ln_ref[2:3, :], ln_ref[3:4, :])

    x1a, h2a = ln1(xs[0], y_fns[0]())
    if len(xs) == 1:
        write(0, ln2(x1a, mlp(h2a, range(n_c), None)))
        return
    yb = y_fns[1]()
    y2a = mlp(h2a, range(0, n_c // 2), None)
    x1b, h2b = ln1(xs[1], yb)
    y2a = mlp(h2a, range(n_c // 2, n_c), y2a)
    y2b = mlp(h2b, range(0, n_c // 2), None)
    write(0, ln2(x1a, y2a))
    y2b = mlp(h2b, range(n_c // 2, n_c), y2b)
    write(1, ln2(x1b, y2b))


def _tiles_per_step(n_tiles):
    return 2 if n_tiles % 2 == 0 else 1


def _mod_row_map(l, n_batch, kind):
    if kind == "both":
        return lambda b, i: (l, jnp.where(i == 0, n_batch, b), 0, 0)
    if kind == "context":
        return lambda b, i: (l, n_batch, 0, 0)
    return lambda b, i: (l, b, 0, 0)


def _vt_row_layout():
    blocks, base = [], 0
    for n in [A_VDIM] * A_HEADS + [HEAD_DIM] * B_KV_HEADS:
        blocks.append((base, n))
        base += n + ONES_ROWS
    return blocks, base


VT_BLOCKS, VT_ROWS = _vt_row_layout()


def _rope(z, cos, s_up, s_dn):
    n = z.shape[-1]
    return z * cos + pltpu.roll(z, n - ROPE_FREQS, 1) * s_up + pltpu.roll(z, ROPE_FREQS, 1) * s_dn


def _group_mean_sq(z, gsum_ref):
    z2 = z * z
    hi = z2.astype(BF16)
    lo = (z2 - hi.astype(F32)).astype(BF16)
    g = gsum_ref[...]
    ss = jnp.dot(hi, g, preferred_element_type=F32) + jnp.dot(lo, g, preferred_element_type=F32)
    return ss * (1.0 / HEAD_DIM)


def _proj_kernel(xc_ref, xl_ref, mod_ref, w_ref, cos_ref, sup_ref, sdn_ref, qg_ref, kg_ref, gsum_ref, msel_ref,
                 q_ref, k_ref, vt_ref, qn_ref, kn_ref):
    h = (_read_stream(xc_ref, xl_ref) * (1.0 + mod_ref[1:2, :]) + mod_ref[0:1, :]).astype(BF16)
    cos, s_up, s_dn = cos_ref[...], sup_ref[...], sdn_ref[...]
    scale = HEAD_DIM ** -0.5 * math.log2(math.e)

    def lane_groups(z, fn):
        return [fn(z[:, g * LANES:(g + 1) * LANES]) for g in range(z.shape[1] // LANES)]

    def rope_all(z, mul):
        return jnp.concatenate(lane_groups(z, lambda t: _rope(t, cos, s_up, s_dn) * mul), axis=1)

    def norm_all(z, gain_ref):
        gain = gain_ref[...]
        return jnp.concatenate(
            lane_groups(z, lambda t: t * lax.rsqrt(_group_mean_sq(t, gsum_ref) + EPS) * gain), axis=1)

    aq = rope_all(jnp.dot(h, w_ref[:, 0:512], preferred_element_type=F32), scale)
    q_ref[:, 0:512] = aq.astype(BF16)
    bq = rope_all(norm_all(jnp.dot(h, w_ref[:, 512:1024], preferred_element_type=F32), qg_ref), scale)
    q_ref[:, 512:1024] = bq.astype(BF16)
    ak = rope_all(jnp.dot(h, w_ref[:, 1024:1536], preferred_element_type=F32), 1.0)
    k_ref[:, 0:512] = ak.astype(BF16)
    bk = rope_all(norm_all(jnp.dot(h, w_ref[:, 1536:1664], preferred_element_type=F32), kg_ref), 1.0)
    k_ref[:, 512:640] = bk.astype(BF16)

    q_sq = (aq * aq).astype(BF16)
    qn_ref[...] = lax.dot_general(msel_ref[...], q_sq, (((1,), (1,)), ((), ())), preferred_element_type=F32)
    k_sq = (ak * ak).astype(BF16)
    g = gsum_ref[...]
    k_n2 = jnp.concatenate(lane_groups(k_sq, lambda t: jnp.dot(t, g, preferred_element_type=F32)), axis=1)
    k_max = jnp.broadcast_to(jnp.max(k_n2, axis=0, keepdims=True), kn_ref.shape)

    v = jnp.dot(h, w_ref[:, 1664:2304], preferred_element_type=F32)
    vt = v.T
    ones = jnp.ones((ONES_ROWS, TILE), F32)
    pieces, col = [], 0
    for _, n in VT_BLOCKS:
        pieces += [vt[col:col + n], ones]
        col += n
    vt_ref[...] = jnp.concatenate(pieces, axis=0).astype(BF16)

    @pl.when(pl.program_id(1) == 0)
    def _():
        kn_ref[...] = k_max

    @pl.when(pl.program_id(1) != 0)
    def _():
        kn_ref[...] = jnp.maximum(kn_ref[...], k_max)


def _even_projection(xs, mod, l, w_in, e, cos_t, sup_t, sdn_t, qg, kg, gsum, msel, n_batch):
    x_specs = _stream_specs(*xs)
    bsz, d = xs[0].shape[0], xs[0].shape[-1]
    t_all = xs[0].shape[1] + xs[1].shape[1]
    nt = t_all // TILE
    return pl.pallas_call(
        _proj_kernel,
        grid=(bsz, nt),
        in_specs=x_specs + [
                  pl.BlockSpec((None, None, N_MOD, d), _mod_row_map(l, n_batch, "both")),
                  _layer_spec(w_in, e),
                  pl.BlockSpec((TILE, LANES), lambda b, i: (i, 0)),
                  pl.BlockSpec((TILE, LANES), lambda b, i: (i, 0)),
                  pl.BlockSpec((TILE, LANES), lambda b, i: (i, 0)),
                  _const_spec(qg.shape), _const_spec(kg.shape), _const_spec(gsum.shape),
                  _const_spec(msel.shape)],
        out_specs=[pl.BlockSpec((None, TILE, Q_COLS), lambda b, i: (b, i, 0)),
                   pl.BlockSpec((None, TILE, K_COLS), lambda b, i: (b, i, 0)),
                   pl.BlockSpec((None, VT_ROWS, TILE), lambda b, i: (b, 0, i)),
                   pl.BlockSpec((None, N_DIFF_MAPS, TILE), lambda b, i: (b, 0, i)),
                   pl.BlockSpec((None, SUBLANES, DIFF_K_COLS), lambda b, i: (b, 0, 0))],
        out_shape=[jax.ShapeDtypeStruct((bsz, t_all, Q_COLS), BF16),
                   jax.ShapeDtypeStruct((bsz, t_all, K_COLS), BF16),
                   jax.ShapeDtypeStruct((bsz, VT_ROWS, t_all), BF16),
                   jax.ShapeDtypeStruct((bsz, N_DIFF_MAPS, t_all), F32),
                   jax.ShapeDtypeStruct((bsz, SUBLANES, DIFF_K_COLS), F32)],
        compiler_params=_params("arbitrary", "arbitrary"),
    )(*xs, mod, w_in, cos_t, sup_t, sdn_t, qg, kg, gsum, msel)


def _key_chunks(n_keys):
    sizes = ([n_keys % KEY_CHUNK] if n_keys % KEY_CHUNK else []) + [KEY_CHUNK] * (n_keys // KEY_CHUNK)
    starts = np.cumsum([0] + sizes[:-1])
    return [(int(a), int(n)) for a, n in zip(starts, sizes)]


def _attn_kernel(q_ref, k_ref, vt_ref, qn_ref, kn_ref, qg_ref, kg_ref, lamv_ref, subg_ref, o_ref, ot_ref, p0_ref, p1_ref,
                 mx_ref, *, lam_init):
    p_refs = (p0_ref, p1_ref)
    chunks = _key_chunks(k_ref.shape[0])
    upper_rows = lax.broadcasted_iota(jnp.int32, (LANES, TILE), 0) >= HEAD_DIM
    zero = jnp.zeros((LANES, TILE), BF16)

    lv = lamv_ref[...]
    lam = (jnp.exp(jnp.sum(lv[0:1] * lv[1:2], axis=-1, keepdims=True))
           - jnp.exp(jnp.sum(lv[2:3] * lv[3:4], axis=-1, keepdims=True)) + lam_init)
    subg = subg_ref[...]

    q0 = A_HEADS * LANES
    pairs = [(slice(hd * LANES, (hd + 1) * LANES), slice(hd * LANES, (hd + 1) * LANES),
              VT_BLOCKS[hd], VT_BLOCKS[hd]) for hd in range(A_HEADS)]
    pairs += [(slice(q0 + j * LANES, q0 + (j + 1) * LANES), slice(q0, q0 + LANES),
               VT_BLOCKS[A_HEADS], VT_BLOCKS[A_HEADS + 1]) for j in range(B_HEADS // 2)]
    n_pairs = len(pairs)

    q_t = q_ref[...].astype(F32).T

    def masked_q(i):
        qt = q_t[pairs[i][0], :].astype(BF16)
        return jnp.concatenate([jnp.where(upper_rows, zero, qt), jnp.where(upper_rows, qt, zero)], axis=1)

    def estimate(i):
        if i >= A_HEADS:
            bound = (jnp.max(jnp.abs(qg_ref[...]), axis=-1, keepdims=True)
                     * jnp.max(jnp.abs(kg_ref[...]), axis=-1, keepdims=True)
                     * (HEAD_DIM ** 0.5 * math.log2(math.e) * SHIFT_MARGIN))
            return jnp.broadcast_to(bound, (1, 2 * TILE))
        k_lo = pairs[i][1].start
        halves = [jnp.sqrt(qn_ref[2 * i + half:2 * i + half + 1, :]
                           * kn_ref[0:1, k_lo + half * HEAD_DIM:k_lo + half * HEAD_DIM + 1])
                  for half in range(2)]
        return jnp.concatenate(halves, axis=1) * SHIFT_MARGIN

    def score_chunk(i, qcat, start, size, shift, m8):
        s = jnp.dot(k_ref[start:start + size, pairs[i][1]], qcat, preferred_element_type=F32)
        p_refs[i % 2][start:start + size, :] = jnp.exp2(s - shift).astype(BF16)
        return jnp.maximum(m8, jnp.max(s.reshape(size // SUBLANES, SUBLANES, 2 * TILE), axis=0))

    def value_chunk(i, start, size, accs):
        _, _, blk_lo, blk_hi = pairs[i]
        p_ref = p_refs[i % 2]
        if blk_lo == blk_hi:
            base, n = blk_lo
            both = jnp.dot(vt_ref[base:base + n + ONES_ROWS, start:start + size], p_ref[start:start + size, :],
                           preferred_element_type=F32)
            parts = [both[:, :TILE], both[:, TILE:]]
        else:
            parts = [jnp.dot(vt_ref[base:base + n + ONES_ROWS, start:start + size],
                             p_ref[start:start + size, half * TILE:(half + 1) * TILE],
                             preferred_element_type=F32)
                     for half, (base, n) in enumerate((blk_lo, blk_hi))]
        return parts if accs is None else [a + b for a, b in zip(accs, parts)]

    def finish(i, accs):
        dens = [acc[n:n + 1] for (_, n), acc in zip(pairs[i][2:], accs)]
        o_lo, o_hi = [acc[:n] * (1.0 / den) for (_, n), acc, den in zip(pairs[i][2:], accs, dens)]
        if i < A_HEADS:
            o = o_lo - lam * o_hi
            ms = jnp.mean(o * o, axis=0, keepdims=True)
            ot_ref[i * A_VDIM:(i + 1) * A_VDIM, :] = o * lax.rsqrt(ms + EPS) * subg * (1.0 - lam_init)
        else:
            base = q0 + (i - A_HEADS) * LANES
            ot_ref[base:base + HEAD_DIM, :] = o_lo
            ot_ref[base + HEAD_DIM:base + LANES, :] = o_hi
        return dens

    mx_ref[...] = jnp.zeros_like(mx_ref)

    def attempt(carry):
        n_done, _ = carry
        den_min, den_max = None, None
        for t in range(n_pairs + 1):
            if t < n_pairs:
                qcat = masked_q(t)
                exact = jnp.max(mx_ref[t], axis=0, keepdims=True)
                shift = jnp.where(n_done == 0, estimate(t), exact)
                m8 = jnp.full((SUBLANES, 2 * TILE), NEG_BIG, F32)
            accs = None
            for start, size in chunks:
                if t < n_pairs:
                    m8 = score_chunk(t, qcat, start, size, shift, m8)
                if 1 <= t:
                    accs = value_chunk(t - 1, start, size, accs)
            if t < n_pairs:
                mx_ref[t] = m8
            if 1 <= t:
                for den in finish(t - 1, accs):
                    den_min = den if den_min is None else jnp.minimum(den_min, den)
                    den_max = den if den_max is None else jnp.maximum(den_max, den)
        safe = jnp.logical_and(jnp.min(den_min) >= DEN_SAFE_MIN, jnp.max(den_max) <= DEN_SAFE_MAX)
        return n_done + 1, safe

    def again(carry):
        n_done, safe = carry
        return jnp.logical_or(n_done == 0, jnp.logical_and(n_done == 1, jnp.logical_not(safe)))

    lax.while_loop(again, attempt, (jnp.int32(0), jnp.bool_(False)))
    o_ref[...] = ot_ref[...].T.astype(BF16)


def _attention(q_all, k_all, vt_all, qn_all, kn_all, qg, kg, lam_vecs, subg, lam_init, context):
    bsz, t_all, _ = q_all.shape
    n_keys = CTX_LEN if context else t_all
    nq = 1 if context else (t_all - CTX_LEN) // TILE
    q_off = 0 if context else CTX_LEN // TILE
    kern = functools.partial(_attn_kernel, lam_init=lam_init)
    return pl.pallas_call(
        kern,
        grid=(bsz, nq),
        in_specs=[pl.BlockSpec((None, TILE, Q_COLS), lambda b, i: (b, i + q_off, 0)),
                  pl.BlockSpec((None, n_keys, K_COLS), lambda b, i: (b, 0, 0)),
                  pl.BlockSpec((None, VT_ROWS, n_keys), lambda b, i: (b, 0, 0)),
                  pl.BlockSpec((None, N_DIFF_MAPS, TILE), lambda b, i: (b, 0, i + q_off)),
                  pl.BlockSpec((None, SUBLANES, DIFF_K_COLS), lambda b, i: (b, 0, 0)),
                  _const_spec(qg.shape), _const_spec(kg.shape),
                  _const_spec(lam_vecs.shape), _const_spec(subg.shape)],
        out_specs=pl.BlockSpec((None, TILE, Q_COLS), lambda b, i: (b, i, 0)),
        out_shape=jax.ShapeDtypeStruct((bsz, nq * TILE, Q_COLS), BF16),
        scratch_shapes=[pltpu.VMEM((Q_COLS, TILE), F32),
                        pltpu.VMEM((n_keys, 2 * TILE), BF16),
                        pltpu.VMEM((n_keys, 2 * TILE), BF16),
                        pltpu.VMEM((N_MAPS // 2, SUBLANES, 2 * TILE), F32)],
        compiler_params=_params("arbitrary", "arbitrary"),
    )(q_all, k_all, vt_all, qn_all, kn_all, qg, kg, lam_vecs, subg)


def _post_kernel(o_ref, x_ref, mod_ref, ln_ref, wo_ref, w1_ref, w2_ref, out_ref, *, alpha):
    n = o_ref.shape[0] // TILE
    rows = [slice(k * TILE, (k + 1) * TILE) for k in range(n)]

    def write(k, v):
        out_ref[rows[k], :] = v

    y_fns = [lambda r=r: jnp.dot(o_ref[r, :], wo_ref[...], preferred_element_type=F32) for r in rows]
    _post_tiles([x_ref[r, :] for r in rows], y_fns, mod_ref, ln_ref, w1_ref, w2_ref, alpha, write)


def _even_post(o, x, mod, ln, l, w_out, e, w1, w2, n_batch, kind, alpha):
    bsz, t, d = x.shape
    rows = TILE * _tiles_per_step(t // TILE)
    blk = pl.BlockSpec((None, rows, d), lambda b, i: (b, i, 0))
    return pl.pallas_call(
        functools.partial(_post_kernel, alpha=alpha),
        grid=(bsz, t // rows),
        in_specs=[blk, blk,
                  pl.BlockSpec((None, None, N_MOD, d), _mod_row_map(l, n_batch, kind)),
                  _layer_spec(ln, l), _layer_spec(w_out, e), _layer_spec(w1, l), _layer_spec(w2, l)],
        out_specs=blk,
        out_shape=jax.ShapeDtypeStruct((bsz, t, d), F32),
        compiler_params=_params("arbitrary", "arbitrary"),
    )(o, x, mod, ln, w_out, w1, w2)


def _odd_kernel(x_ref, xp_ref, xn_ref, mod_ref, ln_ref, wi_ref, cw_ref, cb_ref, wo_ref, w1_ref, w2_ref,
                out_ref, u_ref, *, alpha):
    n_rows = x_ref.shape[0]
    rows = [slice(k * TILE, (k + 1) * TILE) for k in range(n_rows // TILE)]
    i = pl.program_id(1)
    first = i == 0
    last = i == pl.num_programs(1) - 1
    x = x_ref[...]
    shift, scale1 = mod_ref[0:1, :], 1.0 + mod_ref[1:2, :]
    x_ext = jnp.concatenate([xp_ref[...], x, xn_ref[...]], axis=0)
    h_ext = (x_ext * scale1 + shift).astype(BF16)
    z_c = jnp.dot(h_ext, wi_ref[:, D_MODEL:2 * D_MODEL], preferred_element_type=F32)
    z_x = jnp.dot(h_ext, wi_ref[:, 2 * D_MODEL:3 * D_MODEL], preferred_element_type=F32)
    row = lax.broadcasted_iota(jnp.int32, (n_rows + 2 * HALO, 1), 0)
    live_from = jnp.where(first, HALO, 0)
    live_to = jnp.where(last, HALO + n_rows, n_rows + 2 * HALO)
    live = jnp.logical_and(row >= live_from, row < live_to)
    u_ref[...] = jnp.where(live, z_c * z_x, 0.0)

    def mixer(r):
        lo = HALO + r.start
        conv = (u_ref[lo - 1:lo - 1 + TILE, :] * cw_ref[0:1, :] + u_ref[lo:lo + TILE, :] * cw_ref[1:2, :]
                + u_ref[lo + 1:lo + 1 + TILE, :] * cw_ref[2:3, :] + cb_ref[...])
        h = (x_ref[r, :] * scale1 + shift).astype(BF16)
        z_b = jnp.dot(h, wi_ref[:, 0:D_MODEL], preferred_element_type=F32)
        return jnp.dot((z_b * conv).astype(BF16), wo_ref[...], preferred_element_type=F32)

    def write(k, v):
        out_ref[rows[k], :] = v

    _post_tiles([x_ref[r, :] for r in rows], [lambda r=r: mixer(r) for r in rows],
                mod_ref, ln_ref, w1_ref, w2_ref, alpha, write)


def _odd_layer(x, mod, ln, l, w_in, conv_w, conv_b, w_out, od, w1, w2, n_batch, kind, alpha):
    bsz, t, d = x.shape
    rows = TILE * _tiles_per_step(t // TILE)
    per_step = rows // HALO
    n_halo_blocks = t // HALO
    blk = pl.BlockSpec((None, rows, d), lambda b, i: (b, i, 0))
    return pl.pallas_call(
        functools.partial(_odd_kernel, alpha=alpha),
        grid=(bsz, t // rows),
        in_specs=[blk,
                  pl.BlockSpec((None, HALO, d), lambda b, i: (b, jnp.maximum(i * per_step - 1, 0), 0)),
                  pl.BlockSpec((None, HALO, d),
                               lambda b, i: (b, jnp.minimum((i + 1) * per_step, n_halo_blocks - 1), 0)),
                  pl.BlockSpec((None, None, N_MOD, d), _mod_row_map(l, n_batch, kind)),
                  _layer_spec(ln, l), _layer_spec(w_in, od), _layer_spec(conv_w, od),
                  _layer_spec(conv_b, od), _layer_spec(w_out, od),
                  _layer_spec(w1, l), _layer_spec(w2, l)],
        out_specs=blk,
        out_shape=jax.ShapeDtypeStruct((bsz, t, d), F32),
        scratch_shapes=[pltpu.VMEM((rows + 2 * HALO, d), F32)],
        compiler_params=_params("arbitrary", "arbitrary"),
    )(x, x, x, mod, ln, w_in, conv_w, conv_b, w_out, w1, w2)


def _rope_tables(n_lat):
    t = jnp.arange(n_lat)
    pos = jnp.stack([(t // GRID_W).astype(F32), (t % GRID_W).astype(F32)], axis=-1)
    inv_freq = ROPE_THETA ** (-jnp.arange(ROPE_FREQS, dtype=F32) / ROPE_FREQS)
    ang = pos[:, :, None] * inv_freq
    cos = jnp.repeat(jnp.cos(ang), 2, axis=1).reshape(n_lat, HEAD_DIM)
    sin = jnp.repeat(jnp.sin(ang), 2, axis=1).reshape(n_lat, HEAD_DIM)
    first_half = (np.arange(HEAD_DIM) % (2 * ROPE_FREQS)) < ROPE_FREQS
    s_up = jnp.where(first_half, -sin, 0.0)
    s_dn = jnp.where(first_half, 0.0, sin)
    reps = LANES // HEAD_DIM
    pad = lambda a, fill: jnp.concatenate(
        [jnp.full((CTX_LEN, LANES), fill, F32), jnp.tile(a, (1, reps))], axis=0)
    return pad(cos, 1.0), pad(s_up, 0.0), pad(s_dn, 0.0)


def _gqa_head_order():
    half = B_HEADS // 2
    return [h for j in range(half) for h in (j, j + half)]


def kernel(x, c, ctx, c_ctx, ada_w, ada_b, attn_w_in, attn_w_out, diff_lambda, diff_subln_g, q_norm_g, k_norm_g, conv_w_in, conv_w, conv_b, conv_w_out, mlp_w1, mlp_w2, ln_g, ln_b):
    bsz, n_lat, d = x.shape
    depth = ada_w.shape[0]
    assert d == D_MODEL and ctx.shape[1] == CTX_LEN == TILE and n_lat % TILE == 0
    alpha = (2 * depth) ** 0.25

    n_rows = -(-(bsz + 1) // SUBLANES) * SUBLANES
    c_all = jnp.concatenate([c, c_ctx[None, :], jnp.zeros((n_rows - bsz - 1, d), F32)], axis=0)
    mod = _ada_modulation(c_all, ada_w, ada_b).reshape(depth, n_rows, N_MOD, d)

    cos_t, sup_t, sdn_t = _rope_tables(n_lat)
    order = _gqa_head_order()
    bq_segs = [(1536 + HEAD_DIM * h, 1536 + HEAD_DIM * (h + 1)) for h in order]
    in_segs = [(0, 512)] + bq_segs + [(512, 1024), (2048, 2176), (1024, 1536), (2176, 2304)]
    out_segs = [(0, 512)] + [(512 + HEAD_DIM * h, 512 + HEAD_DIM * (h + 1)) for h in order]
    gsum = jnp.asarray(np.kron(np.eye(LANES // HEAD_DIM), np.ones((HEAD_DIM, HEAD_DIM))), BF16)
    msel = jnp.asarray(np.kron(np.eye(N_DIFF_MAPS), np.ones((1, HEAD_DIM))), BF16)
    ln = jnp.stack([ln_g[:, 0], ln_b[:, 0], ln_g[:, 1], ln_b[:, 1]], axis=1)

    w1_all, w2_all = mlp_w1.astype(BF16), mlp_w2.astype(BF16)
    attn_in = jnp.concatenate([attn_w_in[:, :, a:b] for a, b in in_segs], axis=2).astype(BF16)
    attn_out = jnp.concatenate([attn_w_out[:, a:b, :] for a, b in out_segs], axis=1).astype(BF16)
    conv_in, conv_out = conv_w_in.astype(BF16), conv_w_out.astype(BF16)
    conv_bias = conv_b[:, None, :]
    tile2 = lambda g: jnp.tile(g, LANES // HEAD_DIM)[None, :]

    x_ctx, x_lat = ctx, x
    for l in range(depth):
        ctx_out = any(j % 2 == 0 for j in range(l + 1, depth))
        if l % 2 == 0:
            e = l // 2
            lam_init = 0.8 - 0.6 * math.exp(-0.3 * l)
            qg, kg = tile2(q_norm_g[e]), tile2(k_norm_g[e])
            q_all, k_all, vt_all, qn_all, kn_all = _even_projection(
                (x_ctx, x_lat), mod, l, attn_in, e, cos_t, sup_t, sdn_t, qg, kg, gsum, msel, bsz)
            attend = functools.partial(_attention, q_all, k_all, vt_all, qn_all, kn_all, qg, kg, diff_lambda[e],
                                       diff_subln_g[e][:, None], lam_init)
            post = functools.partial(_even_post, mod=mod, ln=ln, l=l, w_out=attn_out, e=e, w1=w1_all,
                                     w2=w2_all, n_batch=bsz, alpha=alpha)
            x_lat = post(attend(False), x_lat, kind="latent")
            x_ctx = post(attend(True), x_ctx, kind="context") if ctx_out else None
        else:
            od = l // 2
            odd = functools.partial(_odd_layer, mod=mod, ln=ln, l=l, w_in=conv_in, conv_w=conv_w,
                                    conv_b=conv_bias, w_out=conv_out, od=od, w1=w1_all, w2=w2_all,
                                    n_batch=bsz, alpha=alpha)
            x_lat = odd(x_lat, kind="latent")
            x_ctx = odd(x_ctx, kind="context") if ctx_out else None
    return x_lat
```

```python
import functools
import math

import numpy as np
import jax
import jax.numpy as jnp
from jax import lax
from jax.experimental import pallas as pl
from jax.experimental.pallas import tpu as pltpu

F32 = jnp.float32
BF16 = jnp.bfloat16

D_MODEL = 1024
CTX_LEN = 256
GRID_W = 64
HEAD_DIM = 64
A_HEADS = 4
A_VDIM = 2 * HEAD_DIM
B_HEADS = 8
B_KV_HEADS = 2
Q_COLS = 1024
K_COLS = 640
V_COLS = 640
ONES_ROWS = 16
MLP_HIDDEN = 4 * D_MODEL
ROPE_THETA = 10000.0
ROPE_FREQS = HEAD_DIM // 4
N_MOD = 6
EPS = 1e-6

TILE = 256
LANES = 128
SUBLANES = 8
HALO = SUBLANES
HID_CHUNK = 1024
KEY_CHUNK = 512
N_MAPS = 2 * A_HEADS + B_HEADS
N_DIFF_MAPS = 2 * A_HEADS
DIFF_K_COLS = A_HEADS * 2 * HEAD_DIM
SHIFT_MARGIN = 1.03
DEN_SAFE_MIN = 2.0 ** -60
DEN_SAFE_MAX = 2.0 ** 60
NEG_BIG = -1e30
VMEM_LIMIT = 56 * 1024 * 1024


def _const_spec(shape):
    nd = len(shape)
    return pl.BlockSpec(shape, lambda *_: (0,) * nd, pipeline_mode=pl.Buffered(1))


def _layer_spec(stacked, l):
    nd = stacked.ndim
    return pl.BlockSpec((None,) + stacked.shape[1:], lambda *_: (l,) + (0,) * (nd - 1),
                        pipeline_mode=pl.Buffered(1))


def _stream_specs(x_ctx, x_lat):
    d = x_ctx.shape[-1]
    n_ctx = CTX_LEN // TILE
    return [pl.BlockSpec((None, TILE, d), lambda b, i: (b, 0, 0)),
            pl.BlockSpec((None, TILE, d), lambda b, i: (b, jnp.maximum(i - n_ctx, 0), 0))]


def _read_stream(ctx_ref, lat_ref):
    return jnp.where(pl.program_id(1) == 0, ctx_ref[...], lat_ref[...])


def _params(*sem):
    return pltpu.CompilerParams(dimension_semantics=sem, vmem_limit_bytes=VMEM_LIMIT)


def _ada_kernel(c_ref, w_ref, b_ref, o_ref):
    c = c_ref[...]
    act = c * (1.0 / (1.0 + jnp.exp(-c)))
    o_ref[...] = jnp.dot(act.astype(BF16), w_ref[...].astype(BF16),
                         preferred_element_type=F32) + b_ref[...]


def _ada_modulation(c_all, ada_w, ada_b):
    depth, d, n = ada_w.shape
    rows = c_all.shape[0]
    tn = 1536
    return pl.pallas_call(
        _ada_kernel,
        grid=(depth, n // tn),
        in_specs=[pl.BlockSpec((rows, d), lambda l, j: (0, 0)),
                  pl.BlockSpec((None, d, tn), lambda l, j: (l, 0, j)),
                  pl.BlockSpec((None, 1, tn), lambda l, j: (l, 0, j))],
        out_specs=pl.BlockSpec((None, rows, tn), lambda l, j: (l, 0, j)),
        out_shape=jax.ShapeDtypeStruct((depth, rows, n), F32),
        compiler_params=_params("arbitrary", "arbitrary"),
    )(c_all, ada_w, ada_b.reshape(depth, 1, n))


def _layer_norm(v, g, b):
    mu = jnp.mean(v, axis=-1, keepdims=True)
    d = v - mu
    var = jnp.mean(d * d, axis=-1, keepdims=True)
    return d * lax.rsqrt(var + EPS) * g + b


def _post_tiles(xs, y_fns, mod_ref, ln_ref, w1_ref, w2_ref, alpha, write):
    n_c = MLP_HIDDEN // HID_CHUNK

    def ln1(x, y):
        x1 = _layer_norm(alpha * x + mod_ref[2:3, :] * y, ln_ref[0:1, :], ln_ref[1:2, :])
        return x1, (x1 * (1.0 + mod_ref[4:5, :]) + mod_ref[3:4, :]).astype(BF16)

    def mlp(h2, chunks, y2):
        for c in chunks:
            sl = slice(c * HID_CHUNK, (c + 1) * HID_CHUNK)
            hid = jnp.maximum(jnp.dot(h2, w1_ref[:, sl], preferred_element_type=F32), 0.0)
            part = jnp.dot((hid * hid).astype(BF16), w2_ref[sl, :], preferred_element_type=F32)
            y2 = part if y2 is None else y2 + part
        return y2

    def ln2(x1, y2):
        return _layer_norm(alpha * x1 + mod_ref[5:6, :] * y2, ln_ref[2:3, :], ln_ref[3:4, :])

    x1a, h2a = ln1(xs[0], y_fns[0]())
    if len(xs) == 1:
        write(0, ln2(x1a, mlp(h2a, range(n_c), None)))
        return
    yb = y_fns[1]()
    y2a = mlp(h2a, range(0, n_c // 2), None)
    x1b, h2b = ln1(xs[1], yb)
    y2a = mlp(h2a, range(n_c // 2, n_c), y2a)
    y2b = mlp(h2b, range(0, n_c // 2), None)
    write(0, ln2(x1a, y2a))
    y2b = mlp(h2b, range(n_c // 2, n_c), y2b)
    write(1, ln2(x1b, y2b))


def _tiles_per_step(n_tiles):
    return 2 if n_tiles % 2 == 0 else 1


def _mod_row_map(l, n_batch, kind):
    if kind == "both":
        return lambda b, i: (l, jnp.where(i == 0, n_batch, b), 0, 0)
    if kind == "context":
        return lambda b, i: (l, n_batch, 0, 0)
    return lambda b, i: (l, b, 0, 0)


def _vt_row_layout():
    blocks, base = [], 0
    for n in [A_VDIM] * A_HEADS + [HEAD_DIM] * B_KV_HEADS:
        blocks.append((base, n))
        base += n + ONES_ROWS
    return blocks, base


VT_BLOCKS, VT_ROWS = _vt_row_layout()


def _rope(z, cos, s_up, s_dn):
    n = z.shape[-1]
    return z * cos + pltpu.roll(z, n - ROPE_FREQS, 1) * s_up + pltpu.roll(z, ROPE_FREQS, 1) * s_dn


def _group_mean_sq(z, gsum_ref):
    z2 = z * z
    hi = z2.astype(BF16)
    lo = (z2 - hi.astype(F32)).astype(BF16)
    g = gsum_ref[...]
    ss = jnp.dot(hi, g, preferred_element_type=F32) + jnp.dot(lo, g, preferred_element_type=F32)
    return ss * (1.0 / HEAD_DIM)


def _proj_kernel(xc_ref, xl_ref, mod_ref, w_ref, cos_ref, sup_ref, sdn_ref, qg_ref, kg_ref, gsum_ref, msel_ref,
                 q_ref, k_ref, vt_ref, qn_ref, kn_ref):
    h = (_read_stream(xc_ref, xl_ref) * (1.0 + mod_ref[1:2, :]) + mod_ref[0:1, :]).astype(BF16)
    cos, s_up, s_dn = cos_ref[...], sup_ref[...], sdn_ref[...]
    scale = HEAD_DIM ** -0.5 * math.log2(math.e)

    def lane_groups(z, fn):
        return [fn(z[:, g * LANES:(g + 1) * LANES]) for g in range(z.shape[1] // LANES)]

    def rope_all(z, mul):
        return jnp.concatenate(lane_groups(z, lambda t: _rope(t, cos, s_up, s_dn) * mul), axis=1)

    def norm_all(z, gain_ref):
        gain = gain_ref[...]
        return jnp.concatenate(
            lane_groups(z, lambda t: t * lax.rsqrt(_group_mean_sq(t, gsum_ref) + EPS) * gain), axis=1)

    aq = rope_all(jnp.dot(h, w_ref[:, 0:512], preferred_element_type=F32), scale)
    q_ref[:, 0:512] = aq.astype(BF16)
    bq = rope_all(norm_all(jnp.dot(h, w_ref[:, 512:1024], preferred_element_type=F32), qg_ref), scale)
    q_ref[:, 512:1024] = bq.astype(BF16)
    ak = rope_all(jnp.dot(h, w_ref[:, 1024:1536], preferred_element_type=F32), 1.0)
    k_ref[:, 0:512] = ak.astype(BF16)
    bk = rope_all(norm_all(jnp.dot(h, w_ref[:, 1536:1664], preferred_element_type=F32), kg_ref), 1.0)
    k_ref[:, 512:640] = bk.astype(BF16)

    q_sq = (aq * aq).astype(BF16)
    qn_ref[...] = lax.dot_general(msel_ref[...], q_sq, (((1,), (1,)), ((), ())), preferred_element_type=F32)
    k_sq = (ak * ak).astype(BF16)
    g = gsum_ref[...]
    k_n2 = jnp.concatenate(lane_groups(k_sq, lambda t: jnp.dot(t, g, preferred_element_type=F32)), axis=1)
    k_max = jnp.broadcast_to(jnp.max(k_n2, axis=0, keepdims=True), kn_ref.shape)

    v = jnp.dot(h, w_ref[:, 1664:2304], preferred_element_type=F32)
    vt = v.T
    ones = jnp.ones((ONES_ROWS, TILE), F32)
    pieces, col = [], 0
    for _, n in VT_BLOCKS:
        pieces += [vt[col:col + n], ones]
        col += n
    vt_ref[...] = jnp.concatenate(pieces, axis=0).astype(BF16)

    @pl.when(pl.program_id(1) == 0)
    def _():
        kn_ref[...] = k_max

    @pl.when(pl.program_id(1) != 0)
    def _():
        kn_ref[...] = jnp.maximum(kn_ref[...], k_max)


def _even_projection(xs, mod, l, w_in, e, cos_t, sup_t, sdn_t, qg, kg, gsum, msel, n_batch):
    x_specs = _stream_specs(*xs)
    bsz, d = xs[0].shape[0], xs[0].shape[-1]
    t_all = xs[0].shape[1] + xs[1].shape[1]
    nt = t_all // TILE
    return pl.pallas_call(
        _proj_kernel,
        grid=(bsz, nt),
        in_specs=x_specs + [
                  pl.BlockSpec((None, None, N_MOD, d), _mod_row_map(l, n_batch, "both")),
                  _layer_spec(w_in, e),
                  pl.BlockSpec((TILE, LANES), lambda b, i: (i, 0)),
                  pl.BlockSpec((TILE, LANES), lambda b, i: (i, 0)),
                  pl.BlockSpec((TILE, LANES), lambda b, i: (i, 0)),
                  _const_spec(qg.shape), _const_spec(kg.shape), _const_spec(gsum.shape),
                  _const_spec(msel.shape)],
        out_specs=[pl.BlockSpec((None, TILE, Q_COLS), lambda b, i: (b, i, 0)),
                   pl.BlockSpec((None, TILE, K_COLS), lambda b, i: (b, i, 0)),
                   pl.BlockSpec((None, VT_ROWS, TILE), lambda b, i: (b, 0, i)),
                   pl.BlockSpec((None, N_DIFF_MAPS, TILE), lambda b, i: (b, 0, i)),
                   pl.BlockSpec((None, SUBLANES, DIFF_K_COLS), lambda b, i: (b, 0, 0))],
        out_shape=[jax.ShapeDtypeStruct((bsz, t_all, Q_COLS), BF16),
                   jax.ShapeDtypeStruct((bsz, t_all, K_COLS), BF16),
                   jax.ShapeDtypeStruct((bsz, VT_ROWS, t_all), BF16),
                   jax.ShapeDtypeStruct((bsz, N_DIFF_MAPS, t_all), F32),
                   jax.ShapeDtypeStruct((bsz, SUBLANES, DIFF_K_COLS), F32)],
        compiler_params=_params("arbitrary", "arbitrary"),
    )(*xs, mod, w_in, cos_t, sup_t, sdn_t, qg, kg, gsum, msel)


def _key_chunks(n_keys):
    sizes = ([n_keys % KEY_CHUNK] if n_keys % KEY_CHUNK else []) + [KEY_CHUNK] * (n_keys // KEY_CHUNK)
    starts = np.cumsum([0] + sizes[:-1])
    return [(int(a), int(n)) for a, n in zip(starts, sizes)]


def _attn_kernel(q_ref, k_ref, vt_ref, qn_ref, kn_ref, qg_ref, kg_ref, lamv_ref, subg_ref, o_ref, p0_ref, p1_ref,
                 mx_ref, *, lam_init):
    p_refs = (p0_ref, p1_ref)
    chunks = _key_chunks(k_ref.shape[0])
    upper_rows = lax.broadcasted_iota(jnp.int32, (LANES, TILE), 0) >= HEAD_DIM
    zero = jnp.zeros((LANES, TILE), BF16)

    lv = lamv_ref[...]
    lam = (jnp.exp(jnp.sum(lv[0:1] * lv[1:2], axis=-1, keepdims=True))
           - jnp.exp(jnp.sum(lv[2:3] * lv[3:4], axis=-1, keepdims=True)) + lam_init)
    subg = subg_ref[...]

    q0 = A_HEADS * LANES
    pairs = [(slice(hd * LANES, (hd + 1) * LANES), slice(hd * LANES, (hd + 1) * LANES),
              VT_BLOCKS[hd], VT_BLOCKS[hd]) for hd in range(A_HEADS)]
    pairs += [(slice(q0 + j * LANES, q0 + (j + 1) * LANES), slice(q0, q0 + LANES),
               VT_BLOCKS[A_HEADS], VT_BLOCKS[A_HEADS + 1]) for j in range(B_HEADS // 2)]
    n_pairs = len(pairs)

    def masked_q(i):
        qt = q_ref[:, pairs[i][0]].astype(F32).T.astype(BF16)
        return jnp.concatenate([jnp.where(upper_rows, zero, qt), jnp.where(upper_rows, qt, zero)], axis=1)

    def estimate(i):
        if i >= A_HEADS:
            bound = (jnp.max(jnp.abs(qg_ref[...]), axis=-1, keepdims=True)
                     * jnp.max(jnp.abs(kg_ref[...]), axis=-1, keepdims=True)
                     * (HEAD_DIM ** 0.5 * math.log2(math.e) * SHIFT_MARGIN))
            return jnp.broadcast_to(bound, (1, 2 * TILE))
        k_lo = pairs[i][1].start
        halves = [jnp.sqrt(qn_ref[2 * i + half:2 * i + half + 1, :]
                           * kn_ref[0:1, k_lo + half * HEAD_DIM:k_lo + half * HEAD_DIM + 1])
                  for half in range(2)]
        return jnp.concatenate(halves, axis=1) * SHIFT_MARGIN

    def score_chunk(i, qcat, start, size, shift, m8):
        s = jnp.dot(k_ref[start:start + size, pairs[i][1]], qcat, preferred_element_type=F32)
        p_refs[i % 2][start:start + size, :] = jnp.exp2(s - shift).astype(BF16)
        return jnp.maximum(m8, jnp.max(s.reshape(size // SUBLANES, SUBLANES, 2 * TILE), axis=0))

    def value_chunk(i, start, size, accs):
        _, _, blk_lo, blk_hi = pairs[i]
        p_ref = p_refs[i % 2]
        if blk_lo == blk_hi:
            base, n = blk_lo
            both = jnp.dot(vt_ref[base:base + n + ONES_ROWS, start:start + size], p_ref[start:start + size, :],
                           preferred_element_type=F32)
            parts = [both[:, :TILE], both[:, TILE:]]
        else:
            parts = [jnp.dot(vt_ref[base:base + n + ONES_ROWS, start:start + size],
                             p_ref[start:start + size, half * TILE:(half + 1) * TILE],
                             preferred_element_type=F32)
                     for half, (base, n) in enumerate((blk_lo, blk_hi))]
        return parts if accs is None else [a + b for a, b in zip(accs, parts)]

    def finish(i, accs):
        dens = [acc[n:n + 1] for (_, n), acc in zip(pairs[i][2:], accs)]
        o_lo, o_hi = [acc[:n] * (1.0 / den) for (_, n), acc, den in zip(pairs[i][2:], accs, dens)]
        if i < A_HEADS:
            o = o_lo - lam * o_hi
            ms = jnp.mean(o * o, axis=0, keepdims=True)
            o = o * lax.rsqrt(ms + EPS) * subg * (1.0 - lam_init)
        else:
            o = jnp.concatenate([o_lo, o_hi], axis=0)
        o_ref[:, pairs[i][0]] = o.T.astype(BF16)
        return dens

    mx_ref[...] = jnp.zeros_like(mx_ref)

    def attempt(carry):
        n_done, _ = carry
        den_min, den_max = None, None
        for t in range(n_pairs + 1):
            if t < n_pairs:
                qcat = masked_q(t)
                exact = jnp.max(mx_ref[t], axis=0, keepdims=True)
                shift = jnp.where(n_done == 0, estimate(t), exact)
                m8 = jnp.full((SUBLANES, 2 * TILE), NEG_BIG, F32)
            accs = None
            for start, size in chunks:
                if t < n_pairs:
                    m8 = score_chunk(t, qcat, start, size, shift, m8)
                if 1 <= t:
                    accs = value_chunk(t - 1, start, size, accs)
            if t < n_pairs:
                mx_ref[t] = m8
            if 1 <= t:
                for den in finish(t - 1, accs):
                    den_min = den if den_min is None else jnp.minimum(den_min, den)
                    den_max = den if den_max is None else jnp.maximum(den_max, den)
        safe = jnp.logical_and(jnp.min(den_min) >= DEN_SAFE_MIN, jnp.max(den_max) <= DEN_SAFE_MAX)
        return n_done + 1, safe

    def again(carry):
        n_done, safe = carry
        return jnp.logical_or(n_done == 0, jnp.logical_and(n_done == 1, jnp.logical_not(safe)))

    lax.while_loop(again, attempt, (jnp.int32(0), jnp.bool_(False)))


def _attention(q_all, k_all, vt_all, qn_all, kn_all, qg, kg, lam_vecs, subg, lam_init, context):
    bsz, t_all, _ = q_all.shape
    n_keys = CTX_LEN if context else t_all
    nq = 1 if context else (t_all - CTX_LEN) // TILE
    q_off = 0 if context else CTX_LEN // TILE
    kern = functools.partial(_attn_kernel, lam_init=lam_init)
    return pl.pallas_call(
        kern,
        grid=(bsz, nq),
        in_specs=[pl.BlockSpec((None, TILE, Q_COLS), lambda b, i: (b, i + q_off, 0)),
                  pl.BlockSpec((None, n_keys, K_COLS), lambda b, i: (b, 0, 0)),
                  pl.BlockSpec((None, VT_ROWS, n_keys), lambda b, i: (b, 0, 0)),
                  pl.BlockSpec((None, N_DIFF_MAPS, TILE), lambda b, i: (b, 0, i + q_off)),
                  pl.BlockSpec((None, SUBLANES, DIFF_K_COLS), lambda b, i: (b, 0, 0)),
                  _const_spec(qg.shape), _const_spec(kg.shape),
                  _const_spec(lam_vecs.shape), _const_spec(subg.shape)],
        out_specs=pl.BlockSpec((None, TILE, Q_COLS), lambda b, i: (b, i, 0)),
        out_shape=jax.ShapeDtypeStruct((bsz, nq * TILE, Q_COLS), BF16),
        scratch_shapes=[pltpu.VMEM((n_keys, 2 * TILE), BF16),
                        pltpu.VMEM((n_keys, 2 * TILE), BF16),
                        pltpu.VMEM((N_MAPS // 2, SUBLANES, 2 * TILE), F32)],
        compiler_params=_params("arbitrary", "arbitrary"),
    )(q_all, k_all, vt_all, qn_all, kn_all, qg, kg, lam_vecs, subg)


def _post_kernel(o_ref, x_ref, mod_ref, ln_ref, wo_ref, w1_ref, w2_ref, out_ref, *, alpha):
    n = o_ref.shape[0] // TILE
    rows = [slice(k * TILE, (k + 1) * TILE) for k in range(n)]

    def write(k, v):
        out_ref[rows[k], :] = v

    y_fns = [lambda r=r: jnp.dot(o_ref[r, :], wo_ref[...], preferred_element_type=F32) for r in rows]
    _post_tiles([x_ref[r, :] for r in rows], y_fns, mod_ref, ln_ref, w1_ref, w2_ref, alpha, write)


def _even_post(o, x, mod, ln, l, w_out, e, w1, w2, n_batch, kind, alpha):
    bsz, t, d = x.shape
    rows = TILE * _tiles_per_step(t // TILE)
    blk = pl.BlockSpec((None, rows, d), lambda b, i: (b, i, 0))
    return pl.pallas_call(
        functools.partial(_post_kernel, alpha=alpha),
        grid=(bsz, t // rows),
        in_specs=[blk, blk,
                  pl.BlockSpec((None, None, N_MOD, d), _mod_row_map(l, n_batch, kind)),
                  _layer_spec(ln, l), _layer_spec(w_out, e), _layer_spec(w1, l), _layer_spec(w2, l)],
        out_specs=blk,
        out_shape=jax.ShapeDtypeStruct((bsz, t, d), F32),
        compiler_params=_params("arbitrary", "arbitrary"),
    )(o, x, mod, ln, w_out, w1, w2)


def _odd_kernel(x_ref, xp_ref, xn_ref, mod_ref, ln_ref, wi_ref, cw_ref, cb_ref, wo_ref, w1_ref, w2_ref,
                out_ref, u_ref, *, alpha):
    n_rows = x_ref.shape[0]
    rows = [slice(k * TILE, (k + 1) * TILE) for k in range(n_rows // TILE)]
    i = pl.program_id(1)
    first = i == 0
    last = i == pl.num_programs(1) - 1
    x = x_ref[...]
    shift, scale1 = mod_ref[0:1, :], 1.0 + mod_ref[1:2, :]
    x_ext = jnp.concatenate([xp_ref[...], x, xn_ref[...]], axis=0)
    h_ext = (x_ext * scale1 + shift).astype(BF16)
    z_c = jnp.dot(h_ext, wi_ref[:, D_MODEL:2 * D_MODEL], preferred_element_type=F32)
    z_x = jnp.dot(h_ext, wi_ref[:, 2 * D_MODEL:3 * D_MODEL], preferred_element_type=F32)
    row = lax.broadcasted_iota(jnp.int32, (n_rows + 2 * HALO, 1), 0)
    live_from = jnp.where(first, HALO, 0)
    live_to = jnp.where(last, HALO + n_rows, n_rows + 2 * HALO)
    live = jnp.logical_and(row >= live_from, row < live_to)
    u_ref[...] = jnp.where(live, z_c * z_x, 0.0)

    def mixer(r):
        lo = HALO + r.start
        conv = (u_ref[lo - 1:lo - 1 + TILE, :] * cw_ref[0:1, :] + u_ref[lo:lo + TILE, :] * cw_ref[1:2, :]
                + u_ref[lo + 1:lo + 1 + TILE, :] * cw_ref[2:3, :] + cb_ref[...])
        h = (x_ref[r, :] * scale1 + shift).astype(BF16)
        z_b = jnp.dot(h, wi_ref[:, 0:D_MODEL], preferred_element_type=F32)
        return jnp.dot((z_b * conv).astype(BF16), wo_ref[...], preferred_element_type=F32)

    def write(k, v):
        out_ref[rows[k], :] = v

    _post_tiles([x_ref[r, :] for r in rows], [lambda r=r: mixer(r) for r in rows],
                mod_ref, ln_ref, w1_ref, w2_ref, alpha, write)


def _odd_layer(x, mod, ln, l, w_in, conv_w, conv_b, w_out, od, w1, w2, n_batch, kind, alpha):
    bsz, t, d = x.shape
    rows = TILE * _tiles_per_step(t // TILE)
    per_step = rows // HALO
    n_halo_blocks = t // HALO
    blk = pl.BlockSpec((None, rows, d), lambda b, i: (b, i, 0))
    return pl.pallas_call(
        functools.partial(_odd_kernel, alpha=alpha),
        grid=(bsz, t // rows),
        in_specs=[blk,
                  pl.BlockSpec((None, HALO, d), lambda b, i: (b, jnp.maximum(i * per_step - 1, 0), 0)),
                  pl.BlockSpec((None, HALO, d),
                               lambda b, i: (b, jnp.minimum((i + 1) * per_step, n_halo_blocks - 1), 0)),
                  pl.BlockSpec((None, None, N_MOD, d), _mod_row_map(l, n_batch, kind)),
                  _layer_spec(ln, l), _layer_spec(w_in, od), _layer_spec(conv_w, od),
                  _layer_spec(conv_b, od), _layer_spec(w_out, od),
                  _layer_spec(w1, l), _layer_spec(w2, l)],
        out_specs=blk,
        out_shape=jax.ShapeDtypeStruct((bsz, t, d), F32),
        scratch_shapes=[pltpu.VMEM((rows + 2 * HALO, d), F32)],
        compiler_params=_params("arbitrary", "arbitrary"),
    )(x, x, x, mod, ln, w_in, conv_w, conv_b, w_out, w1, w2)


def _rope_tables(n_lat):
    t = jnp.arange(n_lat)
    pos = jnp.stack([(t // GRID_W).astype(F32), (t % GRID_W).astype(F32)], axis=-1)
    inv_freq = ROPE_THETA ** (-jnp.arange(ROPE_FREQS, dtype=F32) / ROPE_FREQS)
    ang = pos[:, :, None] * inv_freq
    cos = jnp.repeat(jnp.cos(ang), 2, axis=1).reshape(n_lat, HEAD_DIM)
    sin = jnp.repeat(jnp.sin(ang), 2, axis=1).reshape(n_lat, HEAD_DIM)
    first_half = (np.arange(HEAD_DIM) % (2 * ROPE_FREQS)) < ROPE_FREQS
    s_up = jnp.where(first_half, -sin, 0.0)
    s_dn = jnp.where(first_half, 0.0, sin)
    reps = LANES // HEAD_DIM
    pad = lambda a, fill: jnp.concatenate(
        [jnp.full((CTX_LEN, LANES), fill, F32), jnp.tile(a, (1, reps))], axis=0)
    return pad(cos, 1.0), pad(s_up, 0.0), pad(s_dn, 0.0)


def _gqa_head_order():
    half = B_HEADS // 2
    return [h for j in range(half) for h in (j, j + half)]


def kernel(x, c, ctx, c_ctx, ada_w, ada_b, attn_w_in, attn_w_out, diff_lambda, diff_subln_g, q_norm_g, k_norm_g, conv_w_in, conv_w, conv_b, conv_w_out, mlp_w1, mlp_w2, ln_g, ln_b):
    bsz, n_lat, d = x.shape
    depth = ada_w.shape[0]
    assert d == D_MODEL and ctx.shape[1] == CTX_LEN == TILE and n_lat % TILE == 0
    alpha = (2 * depth) ** 0.25

    n_rows = -(-(bsz + 1) // SUBLANES) * SUBLANES
    c_all = jnp.concatenate([c, c_ctx[None, :], jnp.zeros((n_rows - bsz - 1, d), F32)], axis=0)
    mod = _ada_modulation(c_all, ada_w, ada_b).reshape(depth, n_rows, N_MOD, d)

    cos_t, sup_t, sdn_t = _rope_tables(n_lat)
    order = _gqa_head_order()
    bq_segs = [(1536 + HEAD_DIM * h, 1536 + HEAD_DIM * (h + 1)) for h in order]
    in_segs = [(0, 512)] + bq_segs + [(512, 1024), (2048, 2176), (1024, 1536), (2176, 2304)]
    out_segs = [(0, 512)] + [(512 + HEAD_DIM * h, 512 + HEAD_DIM * (h + 1)) for h in order]
    gsum = jnp.asarray(np.kron(np.eye(LANES // HEAD_DIM), np.ones((HEAD_DIM, HEAD_DIM))), BF16)
    msel = jnp.asarray(np.kron(np.eye(N_DIFF_MAPS), np.ones((1, HEAD_DIM))), BF16)
    ln = jnp.stack([ln_g[:, 0], ln_b[:, 0], ln_g[:, 1], ln_b[:, 1]], axis=1)

    w1_all, w2_all = mlp_w1.astype(BF16), mlp_w2.astype(BF16)
    attn_in = jnp.concatenate([attn_w_in[:, :, a:b] for a, b in in_segs], axis=2).astype(BF16)
    attn_out = jnp.concatenate([attn_w_out[:, a:b, :] for a, b in out_segs], axis=1).astype(BF16)
    conv_in, conv_out = conv_w_in.astype(BF16), conv_w_out.astype(BF16)
    conv_bias = conv_b[:, None, :]
    tile2 = lambda g: jnp.tile(g, LANES // HEAD_DIM)[None, :]

    x_ctx, x_lat = ctx, x
    for l in range(depth):
        ctx_out = any(j % 2 == 0 for j in range(l + 1, depth))
        if l % 2 == 0:
            e = l // 2
            lam_init = 0.8 - 0.6 * math.exp(-0.3 * l)
            qg, kg = tile2(q_norm_g[e]), tile2(k_norm_g[e])
            q_all, k_all, vt_all, qn_all, kn_all = _even_projection(
                (x_ctx, x_lat), mod, l, attn_in, e, cos_t, sup_t, sdn_t, qg, kg, gsum, msel, bsz)
            attend = functools.partial(_attention, q_all, k_all, vt_all, qn_all, kn_all, qg, kg, diff_lambda[e],
                                       diff_subln_g[e][:, None], lam_init)
            post = functools.partial(_even_post, mod=mod, ln=ln, l=l, w_out=attn_out, e=e, w1=w1_all,
                                     w2=w2_all, n_batch=bsz, alpha=alpha)
            x_lat = post(attend(False), x_lat, kind="latent")
            x_ctx = post(attend(True), x_ctx, kind="context") if ctx_out else None
        else:
            od = l // 2
            odd = functools.partial(_odd_layer, mod=mod, ln=ln, l=l, w_in=conv_in, conv_w=conv_w,
                                    conv_b=conv_bias, w_out=conv_out, od=od, w1=w1_all, w2=w2_all,
                                    n_batch=bsz, alpha=alpha)
            x_lat = odd(x_lat, kind="latent")
            x_ctx = odd(x_ctx, kind="context") if ctx_out else None
    return x_lat
```

```python
import functools
import math

import numpy as np
import jax
import jax.numpy as jnp
from jax import lax
from jax.experimental import pallas as pl
from jax.experimental.pallas import tpu as pltpu

F32 = jnp.float32
BF16 = jnp.bfloat16

D_MODEL = 1024
CTX_LEN = 256
GRID_W = 64
HEAD_DIM = 64
A_HEADS = 4
A_VDIM = 2 * HEAD_DIM
B_HEADS = 8
B_KV_HEADS = 2
Q_COLS = 1024
K_COLS = 640
V_COLS = 640
ONES_ROWS = 16
MLP_HIDDEN = 4 * D_MODEL
ROPE_THETA = 10000.0
ROPE_FREQS = HEAD_DIM // 4
N_MOD = 6
EPS = 1e-6

TILE = 256
LANES = 128
SUBLANES = 8
HALO = SUBLANES
HID_CHUNK = 1024
KEY_CHUNK = 768
N_MAPS = 2 * A_HEADS + B_HEADS
N_DIFF_MAPS = 2 * A_HEADS
DIFF_K_COLS = A_HEADS * 2 * HEAD_DIM
SHIFT_MARGIN = 1.03
DEN_SAFE_MIN = 2.0 ** -60
DEN_SAFE_MAX = 2.0 ** 60
NEG_BIG = -1e30
VMEM_LIMIT = 56 * 1024 * 1024


def _const_spec(shape):
    nd = len(shape)
    return pl.BlockSpec(shape, lambda *_: (0,) * nd, pipeline_mode=pl.Buffered(1))


def _layer_spec(stacked, l):
    nd = stacked.ndim
    return pl.BlockSpec((None,) + stacked.shape[1:], lambda *_: (l,) + (0,) * (nd - 1),
                        pipeline_mode=pl.Buffered(1))


def _stream_specs(x_ctx, x_lat):
    d = x_ctx.shape[-1]
    n_ctx = CTX_LEN // TILE
    return [pl.BlockSpec((None, TILE, d), lambda b, i: (b, 0, 0)),
            pl.BlockSpec((None, TILE, d), lambda b, i: (b, jnp.maximum(i - n_ctx, 0), 0))]


def _read_stream(ctx_ref, lat_ref):
    return jnp.where(pl.program_id(1) == 0, ctx_ref[...], lat_ref[...])


def _params(*sem):
    return pltpu.CompilerParams(dimension_semantics=sem, vmem_limit_bytes=VMEM_LIMIT)


def _ada_kernel(c_ref, w_ref, b_ref, o_ref):
    c = c_ref[...]
    act = c * (1.0 / (1.0 + jnp.exp(-c)))
    o_ref[...] = jnp.dot(act.astype(BF16), w_ref[...].astype(BF16),
                         preferred_element_type=F32) + b_ref[...]


def _ada_modulation(c_all, ada_w, ada_b):
    depth, d, n = ada_w.shape
    rows = c_all.shape[0]
    tn = 1536
    return pl.pallas_call(
        _ada_kernel,
        grid=(depth, n // tn),
        in_specs=[pl.BlockSpec((rows, d), lambda l, j: (0, 0)),
                  pl.BlockSpec((None, d, tn), lambda l, j: (l, 0, j)),
                  pl.BlockSpec((None, 1, tn), lambda l, j: (l, 0, j))],
        out_specs=pl.BlockSpec((None, rows, tn), lambda l, j: (l, 0, j)),
        out_shape=jax.ShapeDtypeStruct((depth, rows, n), F32),
        compiler_params=_params("arbitrary", "arbitrary"),
    )(c_all, ada_w, ada_b.reshape(depth, 1, n))


def _layer_norm(v, g, b):
    mu = jnp.mean(v, axis=-1, keepdims=True)
    d = v - mu
    var = jnp.mean(d * d, axis=-1, keepdims=True)
    return d * lax.rsqrt(var + EPS) * g + b


def _post_tiles(xs, y_fns, mod_ref, ln_ref, w1_ref, w2_ref, alpha, write):
    n_c = MLP_HIDDEN // HID_CHUNK

    def ln1(x, y):
        x1 = _layer_norm(alpha * x + mod_ref[2:3, :] * y, ln_ref[0:1, :], ln_ref[1:2, :])
        return x1, (x1 * (1.0 + mod_ref[4:5, :]) + mod_ref[3:4, :]).astype(BF16)

    def mlp(h2, chunks, y2):
        for c in chunks:
            sl = slice(c * HID_CHUNK, (c + 1) * HID_CHUNK)
            hid = jnp.maximum(jnp.dot(h2, w1_ref[:, sl], preferred_element_type=F32), 0.0)
            part = jnp.dot((hid * hid).astype(BF16), w2_ref[sl, :], preferred_element_type=F32)
            y2 = part if y2 is None else y2 + part
        return y2

    def ln2(x1, y2):
        return _layer_norm(alpha * x1 + mod_ref[5:6, :] * y2, ln_ref[2:3, :], ln_ref[3:4, :])

    x1a, h2a = ln1(xs[0], y_fns[0]())
    if len(xs) == 1:
        write(0, ln2(x1a, mlp(h2a, range(n_c), None)))
        return
    yb = y_fns[1]()
    y2a = mlp(h2a, range(0, n_c // 2), None)
    x1b, h2b = ln1(xs[1], yb)
    y2a = mlp(h2a, range(n_c // 2, n_c), y2a)
    y2b = mlp(h2b, range(0, n_c // 2), None)
    write(0, ln2(x1a, y2a))
    y2b = mlp(h2b, range(n_c // 2, n_c), y2b)
    write(1, ln2(x1b, y2b))


def _tiles_per_step(n_tiles):
    return 2 if n_tiles % 2 == 0 else 1


def _mod_row_map(l, n_batch, kind):
    if kind == "both":
        return lambda b, i: (l, jnp.where(i == 0, n_batch, b), 0, 0)
    if kind == "context":
        return lambda b, i: (l, n_batch, 0, 0)
    return lambda b, i: (l, b, 0, 0)


def _vt_row_layout():
    blocks, base = [], 0
    for n in [A_VDIM] * A_HEADS + [HEAD_DIM] * B_KV_HEADS:
        blocks.append((base, n))
        base += n + ONES_ROWS
    return blocks, base


VT_BLOCKS, VT_ROWS = _vt_row_layout()


def _rope(z, cos, s_up, s_dn):
    n = z.shape[-1]
    return z * cos + pltpu.roll(z, n - ROPE_FREQS, 1) * s_up + pltpu.roll(z, ROPE_FREQS, 1) * s_dn


def _group_mean_sq(z, gsum_ref):
    z2 = z * z
    hi = z2.astype(BF16)
    lo = (z2 - hi.astype(F32)).astype(BF16)
    g = gsum_ref[...]
    ss = jnp.dot(hi, g, preferred_element_type=F32) + jnp.dot(lo, g, preferred_element_type=F32)
    return ss * (1.0 / HEAD_DIM)


def _proj_kernel(xc_ref, xl_ref, mod_ref, w_ref, cos_ref, sup_ref, sdn_ref, qg_ref, kg_ref, gsum_ref, msel_ref,
                 q_ref, k_ref, vt_ref, qn_ref, kn_ref):
    h = (_read_stream(xc_ref, xl_ref) * (1.0 + mod_ref[1:2, :]) + mod_ref[0:1, :]).astype(BF16)
    cos, s_up, s_dn = cos_ref[...], sup_ref[...], sdn_ref[...]
    scale = HEAD_DIM ** -0.5 * math.log2(math.e)

    def lane_groups(z, fn):
        return [fn(z[:, g * LANES:(g + 1) * LANES]) for g in range(z.shape[1] // LANES)]

    def rope_all(z, mul):
        return jnp.concatenate(lane_groups(z, lambda t: _rope(t, cos, s_up, s_dn) * mul), axis=1)

    def norm_all(z, gain_ref):
        gain = gain_ref[...]
        return jnp.concatenate(
            lane_groups(z, lambda t: t * lax.rsqrt(_group_mean_sq(t, gsum_ref) + EPS) * gain), axis=1)

    aq = rope_all(jnp.dot(h, w_ref[:, 0:512], preferred_element_type=F32), scale)
    q_ref[:, 0:512] = aq.astype(BF16)
    bq = rope_all(norm_all(jnp.dot(h, w_ref[:, 512:1024], preferred_element_type=F32), qg_ref), scale)
    q_ref[:, 512:1024] = bq.astype(BF16)
    ak = rope_all(jnp.dot(h, w_ref[:, 1024:1536], preferred_element_type=F32), 1.0)
    k_ref[:, 0:512] = ak.astype(BF16)
    bk = rope_all(norm_all(jnp.dot(h, w_ref[:, 1536:1664], preferred_element_type=F32), kg_ref), 1.0)
    k_ref[:, 512:640] = bk.astype(BF16)

    q_sq = (aq * aq).astype(BF16)
    qn_ref[...] = lax.dot_general(msel_ref[...], q_sq, (((1,), (1,)), ((), ())), preferred_element_type=F32)
    k_sq = (ak * ak).astype(BF16)
    g = gsum_ref[...]
    k_n2 = jnp.concatenate(lane_groups(k_sq, lambda t: jnp.dot(t, g, preferred_element_type=F32)), axis=1)
    k_max = jnp.broadcast_to(jnp.max(k_n2, axis=0, keepdims=True), kn_ref.shape)

    v = jnp.dot(h, w_ref[:, 1664:2304], preferred_element_type=F32)
    vt = v.T
    ones = jnp.ones((ONES_ROWS, TILE), F32)
    pieces, col = [], 0
    for _, n in VT_BLOCKS:
        pieces += [vt[col:col + n], ones]
        col += n
    vt_ref[...] = jnp.concatenate(pieces, axis=0).astype(BF16)

    @pl.when(pl.program_id(1) == 0)
    def _():
        kn_ref[...] = k_max

    @pl.when(pl.program_id(1) != 0)
    def _():
        kn_ref[...] = jnp.maximum(kn_ref[...], k_max)


def _even_projection(xs, mod, l, w_in, e, cos_t, sup_t, sdn_t, qg, kg, gsum, msel, n_batch):
    x_specs = _stream_specs(*xs)
    bsz, d = xs[0].shape[0], xs[0].shape[-1]
    t_all = xs[0].shape[1] + xs[1].shape[1]
    nt = t_all // TILE
    return pl.pallas_call(
        _proj_kernel,
        grid=(bsz, nt),
        in_specs=x_specs + [
                  pl.BlockSpec((None, None, N_MOD, d), _mod_row_map(l, n_batch, "both")),
                  _layer_spec(w_in, e),
                  pl.BlockSpec((TILE, LANES), lambda b, i: (i, 0)),
                  pl.BlockSpec((TILE, LANES), lambda b, i: (i, 0)),
                  pl.BlockSpec((TILE, LANES), lambda b, i: (i, 0)),
                  _const_spec(qg.shape), _const_spec(kg.shape), _const_spec(gsum.shape),
                  _const_spec(msel.shape)],
        out_specs=[pl.BlockSpec((None, TILE, Q_COLS), lambda b, i: (b, i, 0)),
                   pl.BlockSpec((None, TILE, K_COLS), lambda b, i: (b, i, 0)),
                   pl.BlockSpec((None, VT_ROWS, TILE), lambda b, i: (b, 0, i)),
                   pl.BlockSpec((None, N_DIFF_MAPS, TILE), lambda b, i: (b, 0, i)),
                   pl.BlockSpec((None, SUBLANES, DIFF_K_COLS), lambda b, i: (b, 0, 0))],
        out_shape=[jax.ShapeDtypeStruct((bsz, t_all, Q_COLS), BF16),
                   jax.ShapeDtypeStruct((bsz, t_all, K_COLS), BF16),
                   jax.ShapeDtypeStruct((bsz, VT_ROWS, t_all), BF16),
                   jax.ShapeDtypeStruct((bsz, N_DIFF_MAPS, t_all), F32),
                   jax.ShapeDtypeStruct((bsz, SUBLANES, DIFF_K_COLS), F32)],
        compiler_params=_params("arbitrary", "arbitrary"),
    )(*xs, mod, w_in, cos_t, sup_t, sdn_t, qg, kg, gsum, msel)


def _key_chunks(n_keys):
    sizes = ([n_keys % KEY_CHUNK] if n_keys % KEY_CHUNK else []) + [KEY_CHUNK] * (n_keys // KEY_CHUNK)
    starts = np.cumsum([0] + sizes[:-1])
    return [(int(a), int(n)) for a, n in zip(starts, sizes)]


def _attn_kernel(q_ref, k_ref, vt_ref, qn_ref, kn_ref, qg_ref, kg_ref, lamv_ref, subg_ref, o_ref, ot_ref, p0_ref, p1_ref,
                 mx_ref, *, lam_init):
    p_refs = (p0_ref, p1_ref)
    chunks = _key_chunks(k_ref.shape[0])
    upper_rows = lax.broadcasted_iota(jnp.int32, (LANES, TILE), 0) >= HEAD_DIM
    zero = jnp.zeros((LANES, TILE), BF16)

    lv = lamv_ref[...]
    lam = (jnp.exp(jnp.sum(lv[0:1] * lv[1:2], axis=-1, keepdims=True))
           - jnp.exp(jnp.sum(lv[2:3] * lv[3:4], axis=-1, keepdims=True)) + lam_init)
    subg = subg_ref[...]

    q0 = A_HEADS * LANES
    pairs = [(slice(hd * LANES, (hd + 1) * LANES), slice(hd * LANES, (hd + 1) * LANES),
              VT_BLOCKS[hd], VT_BLOCKS[hd]) for hd in range(A_HEADS)]
    pairs += [(slice(q0 + j * LANES, q0 + (j + 1) * LANES), slice(q0, q0 + LANES),
               VT_BLOCKS[A_HEADS], VT_BLOCKS[A_HEADS + 1]) for j in range(B_HEADS // 2)]
    n_pairs = len(pairs)

    q_t = q_ref[...].astype(F32).T

    def masked_q(i):
        qt = q_t[pairs[i][0], :].astype(BF16)
        return jnp.concatenate([jnp.where(upper_rows, zero, qt), jnp.where(upper_rows, qt, zero)], axis=1)

    def estimate(i):
        if i >= A_HEADS:
            bound = (jnp.max(jnp.abs(qg_ref[...]), axis=-1, keepdims=True)
                     * jnp.max(jnp.abs(kg_ref[...]), axis=-1, keepdims=True)
                     * (HEAD_DIM ** 0.5 * math.log2(math.e) * SHIFT_MARGIN))
            return jnp.broadcast_to(bound, (1, 2 * TILE))
        k_lo = pairs[i][1].start
        halves = [jnp.sqrt(qn_ref[2 * i + half:2 * i + half + 1, :]
                           * kn_ref[0:1, k_lo + half * HEAD_DIM:k_lo + half * HEAD_DIM + 1])
                  for half in range(2)]
        return jnp.concatenate(halves, axis=1) * SHIFT_MARGIN

    def score_chunk(i, qcat, start, size, shift, m8):
        s = jnp.dot(k_ref[start:start + size, pairs[i][1]], qcat, preferred_element_type=F32)
        p_refs[i % 2][start:start + size, :] = jnp.exp2(s - shift).astype(BF16)
        return jnp.maximum(m8, jnp.max(s.reshape(size // SUBLANES, SUBLANES, 2 * TILE), axis=0))

    def value_chunk(i, start, size, accs):
        _, _, blk_lo, blk_hi = pairs[i]
        p_ref = p_refs[i % 2]
        if blk_lo == blk_hi:
            base, n = blk_lo
            both = jnp.dot(vt_ref[base:base + n + ONES_ROWS, start:start + size], p_ref[start:start + size, :],
                           preferred_element_type=F32)
            parts = [both[:, :TILE], both[:, TILE:]]
        else:
            parts = [jnp.dot(vt_ref[base:base + n + ONES_ROWS, start:start + size],
                             p_ref[start:start + size, half * TILE:(half + 1) * TILE],
                             preferred_element_type=F32)
                     for half, (base, n) in enumerate((blk_lo, blk_hi))]
        return parts if accs is None else [a + b for a, b in zip(accs, parts)]

    def finish(i, accs):
        dens = [acc[n:n + 1] for (_, n), acc in zip(pairs[i][2:], accs)]
        o_lo, o_hi = [acc[:n] * (1.0 / den) for (_, n), acc, den in zip(pairs[i][2:], accs, dens)]
        if i < A_HEADS:
            o = o_lo - lam * o_hi
            ms = jnp.mean(o * o, axis=0, keepdims=True)
            ot_ref[i * A_VDIM:(i + 1) * A_VDIM, :] = o * lax.rsqrt(ms + EPS) * subg * (1.0 - lam_init)
        else:
            base = q0 + (i - A_HEADS) * LANES
            ot_ref[base:base + HEAD_DIM, :] = o_lo
            ot_ref[base + HEAD_DIM:base + LANES, :] = o_hi
        return dens

    mx_ref[...] = jnp.zeros_like(mx_ref)

    def attempt(carry):
        n_done, _ = carry
        den_min, den_max = None, None
        for t in range(n_pairs + 1):
            if t < n_pairs:
                qcat = masked_q(t)
                exact = jnp.max(mx_ref[t], axis=0, keepdims=True)
                shift = jnp.where(n_done == 0, estimate(t), exact)
                m8 = jnp.full((SUBLANES, 2 * TILE), NEG_BIG, F32)
            accs = None
            for start, size in chunks:
                if t < n_pairs:
                    m8 = score_chunk(t, qcat, start, size, shift, m8)
                if 1 <= t:
                    accs = value_chunk(t - 1, start, size, accs)
            if t < n_pairs:
                mx_ref[t] = m8
            if 1 <= t:
                for den in finish(t - 1, accs):
                    den_min = den if den_min is None else jnp.minimum(den_min, den)
                    den_max = den if den_max is None else jnp.maximum(den_max, den)
        safe = jnp.logical_and(jnp.min(den_min) >= DEN_SAFE_MIN, jnp.max(den_max) <= DEN_SAFE_MAX)
        return n_done + 1, safe

    def again(carry):
        n_done, safe = carry
        return jnp.logical_or(n_done == 0, jnp.logical_and(n_done == 1, jnp.logical_not(safe)))

    lax.while_loop(again, attempt, (jnp.int32(0), jnp.bool_(False)))
    o_ref[...] = ot_ref[...].T.astype(BF16)


def _attention(q_all, k_all, vt_all, qn_all, kn_all, qg, kg, lam_vecs, subg, lam_init, context):
    bsz, t_all, _ = q_all.shape
    n_keys = CTX_LEN if context else t_all
    nq = 1 if context else (t_all - CTX_LEN) // TILE
    q_off = 0 if context else CTX_LEN // TILE
    kern = functools.partial(_attn_kernel, lam_init=lam_init)
    return pl.pallas_call(
        kern,
        grid=(bsz, nq),
        in_specs=[pl.BlockSpec((None, TILE, Q_COLS), lambda b, i: (b, i + q_off, 0)),
                  pl.BlockSpec((None, n_keys, K_COLS), lambda b, i: (b, 0, 0)),
                  pl.BlockSpec((None, VT_ROWS, n_keys), lambda b, i: (b, 0, 0)),
                  pl.BlockSpec((None, N_DIFF_MAPS, TILE), lambda b, i: (b, 0, i + q_off)),
                  pl.BlockSpec((None, SUBLANES, DIFF_K_COLS), lambda b, i: (b, 0, 0)),
                  _const_spec(qg.shape), _const_spec(kg.shape),
                  _const_spec(lam_vecs.shape), _const_spec(subg.shape)],
        out_specs=pl.BlockSpec((None, TILE, Q_COLS), lambda b, i: (b, i, 0)),
        out_shape=jax.ShapeDtypeStruct((bsz, nq * TILE, Q_COLS), BF16),
        scratch_shapes=[pltpu.VMEM((Q_COLS, TILE), F32),
                        pltpu.VMEM((n_keys, 2 * TILE), BF16),
                        pltpu.VMEM((n_keys, 2 * TILE), BF16),
                        pltpu.VMEM((N_MAPS // 2, SUBLANES, 2 * TILE), F32)],
        compiler_params=_params("arbitrary", "arbitrary"),
    )(q_all, k_all, vt_all, qn_all, kn_all, qg, kg, lam_vecs, subg)


def _post_kernel(o_ref, x_ref, mod_ref, ln_ref, wo_ref, w1_ref, w2_ref, out_ref, *, alpha):
    n = o_ref.shape[0] // TILE
    rows = [slice(k * TILE, (k + 1) * TILE) for k in range(n)]

    def write(k, v):
        out_ref[rows[k], :] = v

    y_fns = [lambda r=r: jnp.dot(o_ref[r, :], wo_ref[...], preferred_element_type=F32) for r in rows]
    _post_tiles([x_ref[r, :] for r in rows], y_fns, mod_ref, ln_ref, w1_ref, w2_ref, alpha, write)


def _even_post(o, x, mod, ln, l, w_out, e, w1, w2, n_batch, kind, alpha):
    bsz, t, d = x.shape
    rows = TILE * _tiles_per_step(t // TILE)
    blk = pl.BlockSpec((None, rows, d), lambda b, i: (b, i, 0))
    return pl.pallas_call(
        functools.partial(_post_kernel, alpha=alpha),
        grid=(bsz, t // rows),
        in_specs=[blk, blk,
                  pl.BlockSpec((None, None, N_MOD, d), _mod_row_map(l, n_batch, kind)),
                  _layer_spec(ln, l), _layer_spec(w_out, e), _layer_spec(w1, l), _layer_spec(w2, l)],
        out_specs=blk,
        out_shape=jax.ShapeDtypeStruct((bsz, t, d), F32),
        compiler_params=_params("arbitrary", "arbitrary"),
    )(o, x, mod, ln, w_out, w1, w2)


def _odd_kernel(x_ref, xp_ref, xn_ref, mod_ref, ln_ref, wi_ref, cw_ref, cb_ref, wo_ref, w1_ref, w2_ref,
                out_ref, u_ref, *, alpha):
    n_rows = x_ref.shape[0]
    rows = [slice(k * TILE, (k + 1) * TILE) for k in range(n_rows // TILE)]
    i = pl.program_id(1)
    first = i == 0
    last = i == pl.num_programs(1) - 1
    x = x_ref[...]
    shift, scale1 = mod_ref[0:1, :], 1.0 + mod_ref[1:2, :]
    x_ext = jnp.concatenate([xp_ref[...], x, xn_ref[...]], axis=0)
    h_ext = (x_ext * scale1 + shift).astype(BF16)
    z_c = jnp.dot(h_ext, wi_ref[:, D_MODEL:2 * D_MODEL], preferred_element_type=F32)
    z_x = jnp.dot(h_ext, wi_ref[:, 2 * D_MODEL:3 * D_MODEL], preferred_element_type=F32)
    row = lax.broadcasted_iota(jnp.int32, (n_rows + 2 * HALO, 1), 0)
    live_from = jnp.where(first, HALO, 0)
    live_to = jnp.where(last, HALO + n_rows, n_rows + 2 * HALO)
    live = jnp.logical_and(row >= live_from, row < live_to)
    u_ref[...] = jnp.where(live, z_c * z_x, 0.0)

    def mixer(r):
        lo = HALO + r.start
        conv = (u_ref[lo - 1:lo - 1 + TILE, :] * cw_ref[0:1, :] + u_ref[lo:lo + TILE, :] * cw_ref[1:2, :]
                + u_ref[lo + 1:lo + 1 + TILE, :] * cw_ref[2:3, :] + cb_ref[...])
        h = (x_ref[r, :] * scale1 + shift).astype(BF16)
        z_b = jnp.dot(h, wi_ref[:, 0:D_MODEL], preferred_element_type=F32)
        return jnp.dot((z_b * conv).astype(BF16), wo_ref[...], preferred_element_type=F32)

    def write(k, v):
        out_ref[rows[k], :] = v

    _post_tiles([x_ref[r, :] for r in rows], [lambda r=r: mixer(r) for r in rows],
                mod_ref, ln_ref, w1_ref, w2_ref, alpha, write)


def _odd_layer(x, mod, ln, l, w_in, conv_w, conv_b, w_out, od, w1, w2, n_batch, kind, alpha):
    bsz, t, d = x.shape
    rows = TILE * _tiles_per_step(t // TILE)
    per_step = rows // HALO
    n_halo_blocks = t // HALO
    blk = pl.BlockSpec((None, rows, d), lambda b, i: (b, i, 0))
    return pl.pallas_call(
        functools.partial(_odd_kernel, alpha=alpha),
        grid=(bsz, t // rows),
        in_specs=[blk,
                  pl.BlockSpec((None, HALO, d), lambda b, i: (b, jnp.maximum(i * per_step - 1, 0), 0)),
                  pl.BlockSpec((None, HALO, d),
                               lambda b, i: (b, jnp.minimum((i + 1) * per_step, n_halo_blocks - 1), 0)),
                  pl.BlockSpec((None, None, N_MOD, d), _mod_row_map(l, n_batch, kind)),
                  _layer_spec(ln, l), _layer_spec(w_in, od), _layer_spec(conv_w, od),
                  _layer_spec(conv_b, od), _layer_spec(w_out, od),
                  _layer_spec(w1, l), _layer_spec(w2, l)],
        out_specs=blk,
        out_shape=jax.ShapeDtypeStruct((bsz, t, d), F32),
        scratch_shapes=[pltpu.VMEM((rows + 2 * HALO, d), F32)],
        compiler_params=_params("arbitrary", "arbitrary"),
    )(x, x, x, mod, ln, w_in, conv_w, conv_b, w_out, w1, w2)


def _rope_tables(n_lat):
    t = jnp.arange(n_lat)
    pos = jnp.stack([(t // GRID_W).astype(F32), (t % GRID_W).astype(F32)], axis=-1)
    inv_freq = ROPE_THETA ** (-jnp.arange(ROPE_FREQS, dtype=F32) / ROPE_FREQS)
    ang = pos[:, :, None] * inv_freq
    cos = jnp.repeat(jnp.cos(ang), 2, axis=1).reshape(n_lat, HEAD_DIM)
    sin = jnp.repeat(jnp.sin(ang), 2, axis=1).reshape(n_lat, HEAD_DIM)
    first_half = (np.arange(HEAD_DIM) % (2 * ROPE_FREQS)) < ROPE_FREQS
    s_up = jnp.where(first_half, -sin, 0.0)
    s_dn = jnp.where(first_half, 0.0, sin)
    reps = LANES // HEAD_DIM
    pad = lambda a, fill: jnp.concatenate(
        [jnp.full((CTX_LEN, LANES), fill, F32), jnp.tile(a, (1, reps))], axis=0)
    return pad(cos, 1.0), pad(s_up, 0.0), pad(s_dn, 0.0)


def _gqa_head_order():
    half = B_HEADS // 2
    return [h for j in range(half) for h in (j, j + half)]


def kernel(x, c, ctx, c_ctx, ada_w, ada_b, attn_w_in, attn_w_out, diff_lambda, diff_subln_g, q_norm_g, k_norm_g, conv_w_in, conv_w, conv_b, conv_w_out, mlp_w1, mlp_w2, ln_g, ln_b):
    bsz, n_lat, d = x.shape
    depth = ada_w.shape[0]
    assert d == D_MODEL and ctx.shape[1] == CTX_LEN == TILE and n_lat % TILE == 0
    alpha = (2 * depth) ** 0.25

    n_rows = -(-(bsz + 1) // SUBLANES) * SUBLANES
    c_all = jnp.concatenate([c, c_ctx[None, :], jnp.zeros((n_rows - bsz - 1, d), F32)], axis=0)
    mod = _ada_modulation(c_all, ada_w, ada_b).reshape(depth, n_rows, N_MOD, d)

    cos_t, sup_t, sdn_t = _rope_tables(n_lat)
    order = _gqa_head_order()
    bq_segs = [(1536 + HEAD_DIM * h, 1536 + HEAD_DIM * (h + 1)) for h in order]
    in_segs = [(0, 512)] + bq_segs + [(512, 1024), (2048, 2176), (1024, 1536), (2176, 2304)]
    out_segs = [(0, 512)] + [(512 + HEAD_DIM * h, 512 + HEAD_DIM * (h + 1)) for h in order]
    gsum = jnp.asarray(np.kron(np.eye(LANES // HEAD_DIM), np.ones((HEAD_DIM, HEAD_DIM))), BF16)
    msel = jnp.asarray(np.kron(np.eye(N_DIFF_MAPS), np.ones((1, HEAD_DIM))), BF16)
    ln = jnp.stack([ln_g[:, 0], ln_b[:, 0], ln_g[:, 1], ln_b[:, 1]], axis=1)

    w1_all, w2_all = mlp_w1.astype(BF16), mlp_w2.astype(BF16)
    attn_in = jnp.concatenate([attn_w_in[:, :, a:b] for a, b in in_segs], axis=2).astype(BF16)
    attn_out = jnp.concatenate([attn_w_out[:, a:b, :] for a, b in out_segs], axis=1).astype(BF16)
    conv_in, conv_out = conv_w_in.astype(BF16), conv_w_out.astype(BF16)
    conv_bias = conv_b[:, None, :]
    tile2 = lambda g: jnp.tile(g, LANES // HEAD_DIM)[None, :]

    x_ctx, x_lat = ctx, x
    for l in range(depth):
        ctx_out = any(j % 2 == 0 for j in range(l + 1, depth))
        if l % 2 == 0:
            e = l // 2
            lam_init = 0.8 - 0.6 * math.exp(-0.3 * l)
            qg, kg = tile2(q_norm_g[e]), tile2(k_norm_g[e])
            q_all, k_all, vt_all, qn_all, kn_all = _even_projection(
                (x_ctx, x_lat), mod, l, attn_in, e, cos_t, sup_t, sdn_t, qg, kg, gsum, msel, bsz)
            attend = functools.partial(_attention, q_all, k_all, vt_all, qn_all, kn_all, qg, kg, diff_lambda[e],
                                       diff_subln_g[e][:, None], lam_init)
            post = functools.partial(_even_post, mod=mod, ln=ln, l=l, w_out=attn_out, e=e, w1=w1_all,
                                     w2=w2_all, n_batch=bsz, alpha=alpha)
            x_lat = post(attend(False), x_lat, kind="latent")
            x_ctx = post(attend(True), x_ctx, kind="context") if ctx_out else None
        else:
            od = l // 2
            odd = functools.partial(_odd_layer, mod=mod, ln=ln, l=l, w_in=conv_in, conv_w=conv_w,
                                    conv_b=conv_bias, w_out=conv_out, od=od, w1=w1_all, w2=w2_all,
                                    n_batch=bsz, alpha=alpha)
            x_lat = odd(x_lat, kind="latent")
            x_ctx = odd(x_ctx, kind="context") if ctx_out else None
    return x_lat
```

```python
import functools
import math

import numpy as np
import jax
import jax.numpy as jnp
from jax import lax
from jax.experimental import pallas as pl
from jax.experimental.pallas import tpu as pltpu

F32 = jnp.float32
BF16 = jnp.bfloat16

D_MODEL = 1024
CTX_LEN = 256
GRID_W = 64
HEAD_DIM = 64
A_HEADS = 4
A_VDIM = 2 * HEAD_DIM
B_HEADS = 8
B_KV_HEADS = 2
Q_COLS = 1024
K_COLS = 640
ONES_ROWS = 16
MLP_HIDDEN = 4 * D_MODEL
ROPE_THETA = 10000.0
ROPE_FREQS = HEAD_DIM // 4
N_MOD = 6
EPS = 1e-6

TILE = 256
LANES = 128
SUBLANES = 8
HALO = SUBLANES
HID_CHUNK = 1024
KEY_CHUNK = 512
N_MAPS = 2 * A_HEADS + B_HEADS
N_DIFF_MAPS = 2 * A_HEADS
DIFF_K_COLS = A_HEADS * 2 * HEAD_DIM
SHIFT_MARGIN = 1.03
DEN_SAFE_MIN = 2.0 ** -60
DEN_SAFE_MAX = 2.0 ** 60
NEG_BIG = -1e30
VMEM_LIMIT = 56 * 1024 * 1024


def _const_spec(shape):
    nd = len(shape)
    return pl.BlockSpec(shape, lambda *_: (0,) * nd, pipeline_mode=pl.Buffered(1))


def _layer_spec(stacked, l):
    nd = stacked.ndim
    return pl.BlockSpec((None,) + stacked.shape[1:], lambda *_: (l,) + (0,) * (nd - 1),
                        pipeline_mode=pl.Buffered(1))


def _stream_specs(x_ctx, x_lat):
    d = x_ctx.shape[-1]
    n_ctx = CTX_LEN // TILE
    return [pl.BlockSpec((None, TILE, d), lambda b, i: (b, 0, 0)),
            pl.BlockSpec((None, TILE, d), lambda b, i: (b, jnp.maximum(i - n_ctx, 0), 0))]


def _read_stream(ctx_ref, lat_ref):
    return jnp.where(pl.program_id(1) == 0, ctx_ref[...], lat_ref[...])


def _params(*sem):
    return pltpu.CompilerParams(dimension_semantics=sem, vmem_limit_bytes=VMEM_LIMIT)


def _ada_kernel(c_ref, w_ref, b_ref, o_ref):
    c = c_ref[...]
    act = c * (1.0 / (1.0 + jnp.exp(-c)))
    o_ref[...] = jnp.dot(act.astype(BF16), w_ref[...].astype(BF16),
                         preferred_element_type=F32) + b_ref[...]


def _ada_modulation(c_all, ada_w, ada_b):
    depth, d, n = ada_w.shape
    rows = c_all.shape[0]
    tn = 1536
    return pl.pallas_call(
        _ada_kernel,
        grid=(depth, n // tn),
        in_specs=[pl.BlockSpec((rows, d), lambda l, j: (0, 0)),
                  pl.BlockSpec((None, d, tn), lambda l, j: (l, 0, j)),
                  pl.BlockSpec((None, 1, tn), lambda l, j: (l, 0, j))],
        out_specs=pl.BlockSpec((None, rows, tn), lambda l, j: (l, 0, j)),
        out_shape=jax.ShapeDtypeStruct((depth, rows, n), F32),
        compiler_params=_params("arbitrary", "arbitrary"),
    )(c_all, ada_w, ada_b.reshape(depth, 1, n))


def _layer_norm(v, g, b):
    mu = jnp.mean(v, axis=-1, keepdims=True)
    d = v - mu
    var = jnp.mean(d * d, axis=-1, keepdims=True)
    return d * lax.rsqrt(var + EPS) * g + b


def _post_tiles(xs, y_fns, mod_ref, ln_ref, w1_ref, w2_ref, alpha, write):
    n_c = MLP_HIDDEN // HID_CHUNK

    def ln1(x, y):
        x1 = _layer_norm(alpha * x + mod_ref[2:3, :] * y, ln_ref[0:1, :], ln_ref[1:2, :])
        return x1, (x1 * (1.0 + mod_ref[4:5, :]) + mod_ref[3:4, :]).astype(BF16)

    def mlp(h2, chunks, y2):
        for c in chunks:
            sl = slice(c * HID_CHUNK, (c + 1) * HID_CHUNK)
            hid = jnp.maximum(jnp.dot(h2, w1_ref[:, sl], preferred_element_type=F32), 0.0)
            part = jnp.dot((hid * hid).astype(BF16), w2_ref[sl, :], preferred_element_type=F32)
            y2 = part if y2 is None else y2 + part
        return y2

    def ln2(x1, y2):
        return _layer_norm(alpha * x1 + mod_ref[5:6, :] * y2, ln_ref[2:3, :], ln_ref[3:4, :])

    x1a, h2a = ln1(xs[0], y_fns[0]())
    if len(xs) == 1:
        write(0, ln2(x1a, mlp(h2a, range(n_c), None)))
        return
    yb = y_fns[1]()
    y2a = mlp(h2a, range(0, n_c // 2), None)
    x1b, h2b = ln1(xs[1], yb)
    y2a = mlp(h2a, range(n_c // 2, n_c), y2a)
    y2b = mlp(h2b, range(0, n_c // 2), None)
    write(0, ln2(x1a, y2a))
    y2b = mlp(h2b, range(n_c // 2, n_c), y2b)
    write(1, ln2(x1b, y2b))


def _tiles_per_step(n_tiles):
    return 2 if n_tiles % 2 == 0 else 1


def _mod_row_map(l, n_batch, kind):
    if kind == "both":
        return lambda b, i: (l, jnp.where(i == 0, n_batch, b), 0, 0)
    if kind == "context":
        return lambda b, i: (l, n_batch, 0, 0)
    return lambda b, i: (l, b, 0, 0)


def _vt_row_layout():
    blocks, base = [], 0
    for n in [A_VDIM] * A_HEADS + [HEAD_DIM] * B_KV_HEADS:
        blocks.append((base, n))
        base += n + ONES_ROWS
    return blocks, base


VT_BLOCKS, VT_ROWS = _vt_row_layout()


def _rope(z, cos, s_up, s_dn):
    n = z.shape[-1]
    return z * cos + pltpu.roll(z, n - ROPE_FREQS, 1) * s_up + pltpu.roll(z, ROPE_FREQS, 1) * s_dn


def _group_mean_sq(z, gsum_ref):
    z2 = z * z
    hi = z2.astype(BF16)
    lo = (z2 - hi.astype(F32)).astype(BF16)
    g = gsum_ref[...]
    ss = jnp.dot(hi, g, preferred_element_type=F32) + jnp.dot(lo, g, preferred_element_type=F32)
    return ss * (1.0 / HEAD_DIM)


def _proj_kernel(xc_ref, xl_ref, mod_ref, w_ref, cos_ref, sup_ref, sdn_ref, qg_ref, kg_ref, gsum_ref, msel_ref,
                 q_ref, k_ref, vt_ref, qn_ref, kn_ref):
    h = (_read_stream(xc_ref, xl_ref) * (1.0 + mod_ref[1:2, :]) + mod_ref[0:1, :]).astype(BF16)
    cos, s_up, s_dn = cos_ref[...], sup_ref[...], sdn_ref[...]
    scale = HEAD_DIM ** -0.5 * math.log2(math.e)

    def lane_groups(z, fn):
        return [fn(z[:, g * LANES:(g + 1) * LANES]) for g in range(z.shape[1] // LANES)]

    def rope_all(z, mul):
        return jnp.concatenate(lane_groups(z, lambda t: _rope(t, cos, s_up, s_dn) * mul), axis=1)

    def norm_all(z, gain_ref):
        gain = gain_ref[...]
        return jnp.concatenate(
            lane_groups(z, lambda t: t * lax.rsqrt(_group_mean_sq(t, gsum_ref) + EPS) * gain), axis=1)

    aq = rope_all(jnp.dot(h, w_ref[:, 0:512], preferred_element_type=F32), scale)
    q_ref[:, 0:512] = aq.astype(BF16)
    bq = rope_all(norm_all(jnp.dot(h, w_ref[:, 512:1024], preferred_element_type=F32), qg_ref), scale)
    q_ref[:, 512:1024] = bq.astype(BF16)
    ak = rope_all(jnp.dot(h, w_ref[:, 1024:1536], preferred_element_type=F32), 1.0)
    k_ref[:, 0:512] = ak.astype(BF16)
    bk = rope_all(norm_all(jnp.dot(h, w_ref[:, 1536:1664], preferred_element_type=F32), kg_ref), 1.0)
    k_ref[:, 512:640] = bk.astype(BF16)

    q_sq = (aq * aq).astype(BF16)
    qn_ref[...] = lax.dot_general(msel_ref[...], q_sq, (((1,), (1,)), ((), ())), preferred_element_type=F32)
    k_sq = (ak * ak).astype(BF16)
    g = gsum_ref[...]
    k_n2 = jnp.concatenate(lane_groups(k_sq, lambda t: jnp.dot(t, g, preferred_element_type=F32)), axis=1)
    k_max = jnp.broadcast_to(jnp.max(k_n2, axis=0, keepdims=True), kn_ref.shape)

    v = jnp.dot(h, w_ref[:, 1664:2304], preferred_element_type=F32)
    vt = v.T
    ones = jnp.ones((ONES_ROWS, TILE), F32)
    pieces, col = [], 0
    for _, n in VT_BLOCKS:
        pieces += [vt[col:col + n], ones]
        col += n
    vt_ref[...] = jnp.concatenate(pieces, axis=0).astype(BF16)

    @pl.when(pl.program_id(1) == 0)
    def _():
        kn_ref[...] = k_max

    @pl.when(pl.program_id(1) != 0)
    def _():
        kn_ref[...] = jnp.maximum(kn_ref[...], k_max)


def _even_projection(xs, mod, l, w_in, e, cos_t, sup_t, sdn_t, qg, kg, gsum, msel, n_batch):
    x_specs = _stream_specs(*xs)
    bsz, d = xs[0].shape[0], xs[0].shape[-1]
    t_all = xs[0].shape[1] + xs[1].shape[1]
    nt = t_all // TILE
    return pl.pallas_call(
        _proj_kernel,
        grid=(bsz, nt),
        in_specs=x_specs + [
                  pl.BlockSpec((None, None, N_MOD, d), _mod_row_map(l, n_batch, "both")),
                  _layer_spec(w_in, e),
                  pl.BlockSpec((TILE, LANES), lambda b, i: (i, 0)),
                  pl.BlockSpec((TILE, LANES), lambda b, i: (i, 0)),
                  pl.BlockSpec((TILE, LANES), lambda b, i: (i, 0)),
                  _const_spec(qg.shape), _const_spec(kg.shape), _const_spec(gsum.shape),
                  _const_spec(msel.shape)],
        out_specs=[pl.BlockSpec((None, TILE, Q_COLS), lambda b, i: (b, i, 0)),
                   pl.BlockSpec((None, TILE, K_COLS), lambda b, i: (b, i, 0)),
                   pl.BlockSpec((None, VT_ROWS, TILE), lambda b, i: (b, 0, i)),
                   pl.BlockSpec((None, N_DIFF_MAPS, TILE), lambda b, i: (b, 0, i)),
                   pl.BlockSpec((None, SUBLANES, DIFF_K_COLS), lambda b, i: (b, 0, 0))],
        out_shape=[jax.ShapeDtypeStruct((bsz, t_all, Q_COLS), BF16),
                   jax.ShapeDtypeStruct((bsz, t_all, K_COLS), BF16),
                   jax.ShapeDtypeStruct((bsz, VT_ROWS, t_all), BF16),
                   jax.ShapeDtypeStruct((bsz, N_DIFF_MAPS, t_all), F32),
                   jax.ShapeDtypeStruct((bsz, SUBLANES, DIFF_K_COLS), F32)],
        compiler_params=_params("arbitrary", "arbitrary"),
    )(*xs, mod, w_in, cos_t, sup_t, sdn_t, qg, kg, gsum, msel)


def _key_chunks(n_keys):
    sizes = ([n_keys % KEY_CHUNK] if n_keys % KEY_CHUNK else []) + [KEY_CHUNK] * (n_keys // KEY_CHUNK)
    starts = np.cumsum([0] + sizes[:-1])
    return [(int(a), int(n)) for a, n in zip(starts, sizes)]


def _attn_kernel(q_ref, k_ref, vt_ref, qn_ref, kn_ref, qg_ref, kg_ref, lamv_ref, subg_ref, o_ref, ot_ref, p0_ref, p1_ref,
                 mx_ref, *, lam_init):
    p_refs = (p0_ref, p1_ref)
    chunks = _key_chunks(k_ref.shape[0])
    upper_rows = lax.broadcasted_iota(jnp.int32, (LANES, TILE), 0) >= HEAD_DIM
    zero = jnp.zeros((LANES, TILE), BF16)

    lv = lamv_ref[...]
    lam = (jnp.exp(jnp.sum(lv[0:1] * lv[1:2], axis=-1, keepdims=True))
           - jnp.exp(jnp.sum(lv[2:3] * lv[3:4], axis=-1, keepdims=True)) + lam_init)
    subg = subg_ref[...]

    q0 = A_HEADS * LANES
    pairs = [(slice(hd * LANES, (hd + 1) * LANES), slice(hd * LANES, (hd + 1) * LANES),
              VT_BLOCKS[hd], VT_BLOCKS[hd]) for hd in range(A_HEADS)]
    pairs += [(slice(q0 + j * LANES, q0 + (j + 1) * LANES), slice(q0, q0 + LANES),
               VT_BLOCKS[A_HEADS], VT_BLOCKS[A_HEADS + 1]) for j in range(B_HEADS // 2)]
    n_pairs = len(pairs)

    q_t = q_ref[...].astype(F32).T

    def masked_q(i):
        qt = q_t[pairs[i][0], :].astype(BF16)
        return jnp.concatenate([jnp.where(upper_rows, zero, qt), jnp.where(upper_rows, qt, zero)], axis=1)

    def estimate(i):
        if i >= A_HEADS:
            bound = (jnp.max(jnp.abs(qg_ref[...]), axis=-1, keepdims=True)
                     * jnp.max(jnp.abs(kg_ref[...]), axis=-1, keepdims=True)
                     * (HEAD_DIM ** 0.5 * math.log2(math.e) * SHIFT_MARGIN))
            return jnp.broadcast_to(bound, (1, 2 * TILE))
        k_lo = pairs[i][1].start
        halves = [jnp.sqrt(qn_ref[2 * i + half:2 * i + half + 1, :]
                           * kn_ref[0:1, k_lo + half * HEAD_DIM:k_lo + half * HEAD_DIM + 1])
                  for half in range(2)]
        return jnp.concatenate(halves, axis=1) * SHIFT_MARGIN

    def score_chunk(i, qcat, start, size, shift, m8):
        s = jnp.dot(k_ref[start:start + size, pairs[i][1]], qcat, preferred_element_type=F32)
        p_refs[i % 2][start:start + size, :] = jnp.exp2(s - shift).astype(BF16)
        return jnp.maximum(m8, jnp.max(s.reshape(size // SUBLANES, SUBLANES, 2 * TILE), axis=0))

    def value_chunk(i, start, size, accs):
        _, _, blk_lo, blk_hi = pairs[i]
        p_ref = p_refs[i % 2]
        if blk_lo == blk_hi:
            base, n = blk_lo
            both = jnp.dot(vt_ref[base:base + n + ONES_ROWS, start:start + size], p_ref[start:start + size, :],
                           preferred_element_type=F32)
            parts = [both[:, :TILE], both[:, TILE:]]
        else:
            parts = [jnp.dot(vt_ref[base:base + n + ONES_ROWS, start:start + size],
                             p_ref[start:start + size, half * TILE:(half + 1) * TILE],
                             preferred_element_type=F32)
                     for half, (base, n) in enumerate((blk_lo, blk_hi))]
        return parts if accs is None else [a + b for a, b in zip(accs, parts)]

    def finish(i, accs):
        dens = [acc[n:n + 1] for (_, n), acc in zip(pairs[i][2:], accs)]
        o_lo, o_hi = [acc[:n] * (1.0 / den) for (_, n), acc, den in zip(pairs[i][2:], accs, dens)]
        if i < A_HEADS:
            o = o_lo - lam * o_hi
            ms = jnp.mean(o * o, axis=0, keepdims=True)
            ot_ref[i * A_VDIM:(i + 1) * A_VDIM, :] = o * lax.rsqrt(ms + EPS) * subg * (1.0 - lam_init)
        else:
            base = q0 + (i - A_HEADS) * LANES
            ot_ref[base:base + HEAD_DIM, :] = o_lo
            ot_ref[base + HEAD_DIM:base + LANES, :] = o_hi
        return dens

    mx_ref[...] = jnp.zeros_like(mx_ref)

    def attempt(carry):
        n_done, _ = carry
        den_min, den_max = None, None
        for t in range(n_pairs + 1):
            if t < n_pairs:
                qcat = masked_q(t)
                exact = jnp.max(mx_ref[t], axis=0, keepdims=True)
                shift = jnp.where(n_done == 0, estimate(t), exact)
                m8 = jnp.full((SUBLANES, 2 * TILE), NEG_BIG, F32)
            accs = None
            for start, size in chunks:
                if t < n_pairs:
                    m8 = score_chunk(t, qcat, start, size, shift, m8)
                if 1 <= t:
                    accs = value_chunk(t - 1, start, size, accs)
            if t < n_pairs:
                mx_ref[t] = m8
            if 1 <= t:
                for den in finish(t - 1, accs):
                    den_min = den if den_min is None else jnp.minimum(den_min, den)
                    den_max = den if den_max is None else jnp.maximum(den_max, den)
        safe = jnp.logical_and(jnp.min(den_min) >= DEN_SAFE_MIN, jnp.max(den_max) <= DEN_SAFE_MAX)
        return n_done + 1, safe

    def again(carry):
        n_done, safe = carry
        return jnp.logical_or(n_done == 0, jnp.logical_and(n_done == 1, jnp.logical_not(safe)))

    lax.while_loop(again, attempt, (jnp.int32(0), jnp.bool_(False)))
    o_ref[...] = ot_ref[...].T.astype(BF16)


def _attention(q_all, k_all, vt_all, qn_all, kn_all, qg, kg, lam_vecs, subg, lam_init, context):
    bsz, t_all, _ = q_all.shape
    n_keys = CTX_LEN if context else t_all
    nq = 1 if context else (t_all - CTX_LEN) // TILE
    q_off = 0 if context else CTX_LEN // TILE
    kern = functools.partial(_attn_kernel, lam_init=lam_init)
    return pl.pallas_call(
        kern,
        grid=(bsz, nq),
        in_specs=[pl.BlockSpec((None, TILE, Q_COLS), lambda b, i: (b, i + q_off, 0)),
                  pl.BlockSpec((None, n_keys, K_COLS), lambda b, i: (b, 0, 0)),
                  pl.BlockSpec((None, VT_ROWS, n_keys), lambda b, i: (b, 0, 0)),
                  pl.BlockSpec((None, N_DIFF_MAPS, TILE), lambda b, i: (b, 0, i + q_off)),
                  pl.BlockSpec((None, SUBLANES, DIFF_K_COLS), lambda b, i: (b, 0, 0)),
                  _const_spec(qg.shape), _const_spec(kg.shape),
                  _const_spec(lam_vecs.shape), _const_spec(subg.shape)],
        out_specs=pl.BlockSpec((None, TILE, Q_COLS), lambda b, i: (b, i, 0)),
        out_shape=jax.ShapeDtypeStruct((bsz, nq * TILE, Q_COLS), BF16),
        scratch_shapes=[pltpu.VMEM((Q_COLS, TILE), F32),
                        pltpu.VMEM((n_keys, 2 * TILE), BF16),
                        pltpu.VMEM((n_keys, 2 * TILE), BF16),
                        pltpu.VMEM((N_MAPS // 2, SUBLANES, 2 * TILE), F32)],
        compiler_params=_params("arbitrary", "arbitrary"),
    )(q_all, k_all, vt_all, qn_all, kn_all, qg, kg, lam_vecs, subg)


def _post_kernel(o_ref, x_ref, mod_ref, ln_ref, wo_ref, w1_ref, w2_ref, out_ref, *, alpha):
    n = o_ref.shape[0] // TILE
    rows = [slice(k * TILE, (k + 1) * TILE) for k in range(n)]

    def write(k, v):
        out_ref[rows[k], :] = v

    y_fns = [lambda r=r: jnp.dot(o_ref[r, :], wo_ref[...], preferred_element_type=F32) for r in rows]
    _post_tiles([x_ref[r, :] for r in rows], y_fns, mod_ref, ln_ref, w1_ref, w2_ref, alpha, write)


def _even_post(o, x, mod, ln, l, w_out, e, w1, w2, n_batch, kind, alpha):
    bsz, t, d = x.shape
    rows = TILE * _tiles_per_step(t // TILE)
    blk = pl.BlockSpec((None, rows, d), lambda b, i: (b, i, 0))
    return pl.pallas_call(
        functools.partial(_post_kernel, alpha=alpha),
        grid=(bsz, t // rows),
        in_specs=[blk, blk,
                  pl.BlockSpec((None, None, N_MOD, d), _mod_row_map(l, n_batch, kind)),
                  _layer_spec(ln, l), _layer_spec(w_out, e), _layer_spec(w1, l), _layer_spec(w2, l)],
        out_specs=blk,
        out_shape=jax.ShapeDtypeStruct((bsz, t, d), F32),
        compiler_params=_params("arbitrary", "arbitrary"),
    )(o, x, mod, ln, w_out, w1, w2)


def _odd_kernel(x_ref, xp_ref, xn_ref, mod_ref, ln_ref, wi_ref, cw_ref, cb_ref, wo_ref, w1_ref, w2_ref,
                out_ref, u_ref, *, alpha):
    n_rows = x_ref.shape[0]
    rows = [slice(k * TILE, (k + 1) * TILE) for k in range(n_rows // TILE)]
    i = pl.program_id(1)
    first = i == 0
    last = i == pl.num_programs(1) - 1
    x = x_ref[...]
    shift, scale1 = mod_ref[0:1, :], 1.0 + mod_ref[1:2, :]
    x_ext = jnp.concatenate([xp_ref[...], x, xn_ref[...]], axis=0)
    h_ext = (x_ext * scale1 + shift).astype(BF16)
    z_c = jnp.dot(h_ext, wi_ref[:, D_MODEL:2 * D_MODEL], preferred_element_type=F32)
    z_x = jnp.dot(h_ext, wi_ref[:, 2 * D_MODEL:3 * D_MODEL], preferred_element_type=F32)
    row = lax.broadcasted_iota(jnp.int32, (n_rows + 2 * HALO, 1), 0)
    live_from = jnp.where(first, HALO, 0)
    live_to = jnp.where(last, HALO + n_rows, n_rows + 2 * HALO)
    live = jnp.logical_and(row >= live_from, row < live_to)
    u_ref[...] = jnp.where(live, z_c * z_x, 0.0)

    def mixer(r):
        lo = HALO + r.start
        conv = (u_ref[lo - 1:lo - 1 + TILE, :] * cw_ref[0:1, :] + u_ref[lo:lo + TILE, :] * cw_ref[1:2, :]
                + u_ref[lo + 1:lo + 1 + TILE, :] * cw_ref[2:3, :] + cb_ref[...])
        h = (x_ref[r, :] * scale1 + shift).astype(BF16)
        z_b = jnp.dot(h, wi_ref[:, 0:D_MODEL], preferred_element_type=F32)
        return jnp.dot((z_b * conv).astype(BF16), wo_ref[...], preferred_element_type=F32)

    def write(k, v):
        out_ref[rows[k], :] = v

    _post_tiles([x_ref[r, :] for r in rows], [lambda r=r: mixer(r) for r in rows],
                mod_ref, ln_ref, w1_ref, w2_ref, alpha, write)


def _odd_layer(x, mod, ln, l, w_in, conv_w, conv_b, w_out, od, w1, w2, n_batch, kind, alpha):
    bsz, t, d = x.shape
    rows = TILE * _tiles_per_step(t // TILE)
    per_step = rows // HALO
    n_halo_blocks = t // HALO
    blk = pl.BlockSpec((None, rows, d), lambda b, i: (b, i, 0))
    return pl.pallas_call(
        functools.partial(_odd_kernel, alpha=alpha),
        grid=(bsz, t // rows),
        in_specs=[blk,
                  pl.BlockSpec((None, HALO, d), lambda b, i: (b, jnp.maximum(i * per_step - 1, 0), 0)),
                  pl.BlockSpec((None, HALO, d),
                               lambda b, i: (b, jnp.minimum((i + 1) * per_step, n_halo_blocks - 1), 0)),
                  pl.BlockSpec((None, None, N_MOD, d), _mod_row_map(l, n_batch, kind)),
                  _layer_spec(ln, l), _layer_spec(w_in, od), _layer_spec(conv_w, od),
                  _layer_spec(conv_b, od), _layer_spec(w_out, od),
                  _layer_spec(w1, l), _layer_spec(w2, l)],
        out_specs=blk,
        out_shape=jax.ShapeDtypeStruct((bsz, t, d), F32),
        scratch_shapes=[pltpu.VMEM((rows + 2 * HALO, d), F32)],
        compiler_params=_params("arbitrary", "arbitrary"),
    )(x, x, x, mod, ln, w_in, conv_w, conv_b, w_out, w1, w2)


def _rope_tables(n_lat):
    t = jnp.arange(n_lat)
    pos = jnp.stack([(t // GRID_W).astype(F32), (t % GRID_W).astype(F32)], axis=-1)
    inv_freq = ROPE_THETA ** (-jnp.arange(ROPE_FREQS, dtype=F32) / ROPE_FREQS)
    ang = pos[:, :, None] * inv_freq
    cos = jnp.repeat(jnp.cos(ang), 2, axis=1).reshape(n_lat, HEAD_DIM)
    sin = jnp.repeat(jnp.sin(ang), 2, axis=1).reshape(n_lat, HEAD_DIM)
    first_half = (np.arange(HEAD_DIM) % (2 * ROPE_FREQS)) < ROPE_FREQS
    s_up = jnp.where(first_half, -sin, 0.0)
    s_dn = jnp.where(first_half, 0.0, sin)
    reps = LANES // HEAD_DIM
    pad = lambda a, fill: jnp.concatenate(
        [jnp.full((CTX_LEN, LANES), fill, F32), jnp.tile(a, (1, reps))], axis=0)
    return pad(cos, 1.0), pad(s_up, 0.0), pad(s_dn, 0.0)


def _gqa_head_order():
    half = B_HEADS // 2
    return [h for j in range(half) for h in (j, j + half)]


def kernel(x, c, ctx, c_ctx, ada_w, ada_b, attn_w_in, attn_w_out, diff_lambda, diff_subln_g, q_norm_g, k_norm_g, conv_w_in, conv_w, conv_b, conv_w_out, mlp_w1, mlp_w2, ln_g, ln_b):
    bsz, n_lat, d = x.shape
    depth = ada_w.shape[0]
    assert d == D_MODEL and ctx.shape[1] == CTX_LEN == TILE and n_lat % TILE == 0
    alpha = (2 * depth) ** 0.25

    n_rows = -(-(bsz + 1) // SUBLANES) * SUBLANES
    c_all = jnp.concatenate([c, c_ctx[None, :], jnp.zeros((n_rows - bsz - 1, d), F32)], axis=0)
    mod = _ada_modulation(c_all, ada_w, ada_b).reshape(depth, n_rows, N_MOD, d)

    cos_t, sup_t, sdn_t = _rope_tables(n_lat)
    order = _gqa_head_order()
    bq_segs = [(1536 + HEAD_DIM * h, 1536 + HEAD_DIM * (h + 1)) for h in order]
    in_segs = [(0, 512)] + bq_segs + [(512, 1024), (2048, 2176), (1024, 1536), (2176, 2304)]
    out_segs = [(0, 512)] + [(512 + HEAD_DIM * h, 512 + HEAD_DIM * (h + 1)) for h in order]
    gsum = jnp.asarray(np.kron(np.eye(LANES // HEAD_DIM), np.ones((HEAD_DIM, HEAD_DIM))), BF16)
    msel = jnp.asarray(np.kron(np.eye(N_DIFF_MAPS), np.ones((1, HEAD_DIM))), BF16)
    ln = jnp.stack([ln_g[:, 0], ln_b[:, 0], ln_g[:, 1], ln_b[:, 1]], axis=1)

    w1_all, w2_all = mlp_w1.astype(BF16), mlp_w2.astype(BF16)
    attn_in = jnp.concatenate([attn_w_in[:, :, a:b] for a, b in in_segs], axis=2).astype(BF16)
    attn_out = jnp.concatenate([attn_w_out[:, a:b, :] for a, b in out_segs], axis=1).astype(BF16)
    conv_in, conv_out = conv_w_in.astype(BF16), conv_w_out.astype(BF16)
    conv_bias = conv_b[:, None, :]
    tile2 = lambda g: jnp.tile(g, LANES // HEAD_DIM)[None, :]

    x_ctx, x_lat = ctx, x
    for l in range(depth):
        ctx_out = any(j % 2 == 0 for j in range(l + 1, depth))
        if l % 2 == 0:
            e = l // 2
            lam_init = 0.8 - 0.6 * math.exp(-0.3 * l)
            qg, kg = tile2(q_norm_g[e]), tile2(k_norm_g[e])
            q_all, k_all, vt_all, qn_all, kn_all = _even_projection(
                (x_ctx, x_lat), mod, l, attn_in, e, cos_t, sup_t, sdn_t, qg, kg, gsum, msel, bsz)
            attend = functools.partial(_attention, q_all, k_all, vt_all, qn_all, kn_all, qg, kg, diff_lambda[e],
                                       diff_subln_g[e][:, None], lam_init)
            post = functools.partial(_even_post, mod=mod, ln=ln, l=l, w_out=attn_out, e=e, w1=w1_all,
                                     w2=w2_all, n_batch=bsz, alpha=alpha)
            x_lat = post(attend(False), x_lat, kind="latent")
            x_ctx = post(attend(True), x_ctx, kind="context") if ctx_out else None
        else:
            od = l // 2
            odd = functools.partial(_odd_layer, mod=mod, ln=ln, l=l, w_in=conv_in, conv_w=conv_w,
                                    conv_b=conv_bias, w_out=conv_out, od=od, w1=w1_all, w2=w2_all,
                                    n_batch=bsz, alpha=alpha)
            x_lat = odd(x_lat, kind="latent")
            x_ctx = odd(x_ctx, kind="context") if ctx_out else None
    return x_lat
```
